```python
import math
import jax, jax.numpy as jnp
from jax import lax
import numpy as np

D_MODEL = 2048
BATCH = 2
SEQ = 4096
DEPTH = 2

CHUNK = 64
N_A = max(1, DEPTH // 2)
N_B = DEPTH - N_A
EPS = 1e-6
PLE_DIM = 256
D_FF = 4 * D_MODEL

SSM_EXPAND = 2
D_INNER = SSM_EXPAND * D_MODEL
SSM_HEADDIM = 64
SSM_HEADS = D_INNER // SSM_HEADDIM
SSM_GROUPS = 8
SSM_STATE = 128
CONV_W = 4
GN = SSM_GROUPS * SSM_STATE
CONV_DIM = D_INNER + 2 * GN
IN_PROJ_DIM = D_INNER + CONV_DIM + SSM_HEADS

MLA_HEADS = 16
QK_NOPE = 128
QK_ROPE = 64
QK_HEAD = QK_NOPE + QK_ROPE
V_HEAD = 128
Q_LORA = 512
KV_LORA = 512
ROPE_THETA = 10000.0
Q_BLOCK = 128

kernel_name = 'yoco_mamba2_mla_hybrid'


def rmsnorm(x, g):
    xf = x.astype(jnp.float32)
    y = xf * lax.rsqrt(jnp.mean(xf * xf, axis=-1, keepdims=True) + EPS)
    return (y * g.astype(jnp.float32)).astype(x.dtype)


def rope(x, positions):
    half = x.shape[-1] // 2
    freqs = ROPE_THETA ** (-jnp.arange(half, dtype=jnp.float32) / half)
    ang = positions.astype(jnp.float32)[..., None] * freqs
    ang = ang.reshape(ang.shape[:2] + (1,) * (x.ndim - 3) + (half,))
    cos, sin = jnp.cos(ang), jnp.sin(ang)
    xf = x.astype(jnp.float32)
    x1, x2 = xf[..., :half], xf[..., half:]
    return jnp.concatenate([x1 * cos - x2 * sin, x2 * cos + x1 * sin], axis=-1).astype(x.dtype)


def causal_depthwise_conv(u, w, b):
    out = lax.conv_general_dilated(
        u, w[:, None, :].astype(u.dtype), window_strides=(1,), padding=[(CONV_W - 1, 0)],
        dimension_numbers=('NWC', 'WIO', 'NWC'), feature_group_count=u.shape[-1])
    return out + b.astype(u.dtype)


def ssd_chunked_scan(x, a, bm, cm):
    bsz, s, h, p = x.shape
    g, n = bm.shape[-2:]
    hg = h // g
    nc = s // CHUNK
    x = x.reshape(bsz, nc, CHUNK, g, hg, p)
    a = a.reshape(bsz, nc, CHUNK, g, hg).transpose(0, 1, 3, 4, 2)
    bm = bm.reshape(bsz, nc, CHUNK, g, n)
    cm = cm.reshape(bsz, nc, CHUNK, g, n)
    a_cs = jnp.cumsum(a, axis=-1)
    causal = jnp.tril(jnp.ones((CHUNK, CHUNK), dtype=bool))
    seg = a_cs[..., :, None] - a_cs[..., None, :]
    decay = jnp.exp(jnp.where(causal, seg, -jnp.inf))
    cb = jnp.einsum('bclgn,bcsgn->bcgls', cm, bm)
    y_diag = jnp.einsum('bcgls,bcghls,bcsghp->bclghp', cb, decay, x)
    decay_states = jnp.exp(a_cs[..., -1:] - a_cs)
    states = jnp.einsum('bclgn,bcghl,bclghp->bcghpn', bm, decay_states, x)
    chunk_decay = jnp.exp(a_cs[..., -1])

    def step(carry, inp):
        st, dec = inp
        return carry * dec[..., None, None] + st, carry

    init = jnp.zeros((bsz, g, hg, p, n), dtype=x.dtype)
    _, prev_states = lax.scan(step, init, (jnp.moveaxis(states, 1, 0), jnp.moveaxis(chunk_decay, 1, 0)))
    prev_states = jnp.moveaxis(prev_states, 0, 1)
    y_off = jnp.einsum('bclgn,bcghpn,bcghl->bclghp', cm, prev_states, jnp.exp(a_cs))
    return (y_diag + y_off).reshape(bsz, s, h, p)


def mamba2_mixer(hn, w_in, conv_w, conv_b, dt_bias, a_log, d_skip, norm_g, w_out):
    bsz, s, _ = hn.shape
    zxbcdt = hn @ w_in
    z = zxbcdt[..., :D_INNER]
    xbc = zxbcdt[..., D_INNER:D_INNER + CONV_DIM]
    dt = zxbcdt[..., D_INNER + CONV_DIM:]
    xbc = jax.nn.silu(causal_depthwise_conv(xbc, conv_w, conv_b))
    xs = xbc[..., :D_INNER].reshape(bsz, s, SSM_HEADS, SSM_HEADDIM).astype(jnp.float32)
    bm = xbc[..., D_INNER:D_INNER + GN].reshape(bsz, s, SSM_GROUPS, SSM_STATE).astype(jnp.float32)
    cm = xbc[..., D_INNER + GN:].reshape(bsz, s, SSM_GROUPS, SSM_STATE).astype(jnp.float32)
    dt = jax.nn.softplus(dt.astype(jnp.float32) + dt_bias.astype(jnp.float32))
    a = -jnp.exp(a_log.astype(jnp.float32))
    y = ssd_chunked_scan(xs * dt[..., None], dt * a, bm, cm)
    y = y + d_skip.astype(jnp.float32)[:, None] * xs
    y = y.reshape(bsz, s, D_INNER) * jax.nn.silu(z.astype(jnp.float32))
    y = rmsnorm(y.reshape(bsz, s, SSM_GROUPS, D_INNER // SSM_GROUPS),
                norm_g.reshape(SSM_GROUPS, D_INNER // SSM_GROUPS)).reshape(bsz, s, D_INNER)
    return y.astype(hn.dtype) @ w_out


def mla_shared_kv(h, kv_in_norm, w_dkv, ckv_norm, w_kr, w_uk, w_uv, k_nope_norm, k_rope_norm, positions):
    bsz, s, _ = h.shape
    hn = rmsnorm(h, kv_in_norm)
    c_kv = rmsnorm(hn @ w_dkv, ckv_norm)
    k_rope = rope(rmsnorm(hn @ w_kr, k_rope_norm), positions)
    k_nope = rmsnorm((c_kv @ w_uk).reshape(bsz, s, MLA_HEADS, QK_NOPE), k_nope_norm)
    v = (c_kv @ w_uv).reshape(bsz, s, MLA_HEADS, V_HEAD)
    return k_nope, k_rope, v


def chunk_causal_attention(q_nope, q_rope, k_nope, k_rope, v):
    bsz, s, h, _ = q_nope.shape
    nb = s // Q_BLOCK
    scale = QK_HEAD ** -0.5
    qn = jnp.moveaxis(q_nope.reshape(bsz, nb, Q_BLOCK, h, QK_NOPE), 1, 0)
    qr = jnp.moveaxis(q_rope.reshape(bsz, nb, Q_BLOCK, h, QK_ROPE), 1, 0)
    key_chunk = jnp.arange(s) // CHUNK

    def one_block(args):
        i, qn_b, qr_b = args
        sc = (jnp.einsum('bqhd,bkhd->bhqk', qn_b, k_nope)
              + jnp.einsum('bqhr,bkr->bhqk', qr_b, k_rope)).astype(jnp.float32) * scale
        q_chunk = (i * Q_BLOCK + jnp.arange(Q_BLOCK)) // CHUNK
        mask = key_chunk[None, :] <= q_chunk[:, None]
        probs = jax.nn.softmax(jnp.where(mask, sc, -jnp.inf), axis=-1)
        return jnp.einsum('bhqk,bkhd->bqhd', probs.astype(v.dtype), v)

    out = lax.map(one_block, (jnp.arange(nb), qn, qr))
    return jnp.moveaxis(out, 0, 1).reshape(bsz, s, h * V_HEAD)


def mla_mixer(hn, w_dq, cq_norm, w_uq, q_nope_norm, q_rope_norm, w_o, k_nope, k_rope, v, positions):
    bsz, s, _ = hn.shape
    c_q = rmsnorm(hn @ w_dq, cq_norm)
    q = (c_q @ w_uq).reshape(bsz, s, MLA_HEADS, QK_HEAD)
    q_nope = rmsnorm(q[..., :QK_NOPE], q_nope_norm)
    q_rope = rope(rmsnorm(q[..., QK_NOPE:], q_rope_norm), positions)
    return chunk_causal_attention(q_nope, q_rope, k_nope, k_rope, v) @ w_o


def sq_relu_mlp(hn, w1, w2):
    return jnp.square(jax.nn.relu(hn @ w1)) @ w2


def per_layer_input(h, p_i, g, w_gate, w_proj):
    gate = jax.nn.sigmoid(rmsnorm(h, g) @ w_gate)
    return h + gate * (p_i @ w_proj)


def setup_inputs(seed: int = 0) -> dict:
    key = jax.random.key(seed)
    keys = list(jax.random.split(key, 48))
    f32 = jnp.float32

    def nk():
        return keys.pop()

    def nrm(shape, fan_in, scale=1.0):
        return jax.random.normal(nk(), shape, f32) * (scale * fan_in ** -0.5)

    def gain(shape):
        return 1.0 + 0.05 * jax.random.normal(nk(), shape, f32)

    out_s = (2.0 * DEPTH) ** -0.5
    x = jax.random.normal(nk(), (BATCH, SEQ, D_MODEL), f32)
    p = jax.random.normal(nk(), (DEPTH, BATCH, SEQ, PLE_DIM), f32)
    positions = jnp.tile(jnp.arange(SEQ, dtype=jnp.int32)[None, :], (BATCH, 1))
    dt = jnp.exp(jax.random.uniform(nk(), (N_A, SSM_HEADS), f32, math.log(1e-3), math.log(1e-1)))
    return {
        'x': x,
        'p': p,
        'positions': positions,
        'ln_mix': gain((DEPTH, D_MODEL)),
        'ln_mlp': gain((DEPTH, D_MODEL)),
        'mlp_w1': nrm((DEPTH, D_MODEL, D_FF), D_MODEL),
        'mlp_w2': nrm((DEPTH, D_FF, D_MODEL), D_FF, out_s),
        'ple_norm': gain((DEPTH, D_MODEL)),
        'ple_gate_w': nrm((DEPTH, D_MODEL, D_MODEL), D_MODEL),
        'ple_proj_w': nrm((DEPTH, PLE_DIM, D_MODEL), PLE_DIM, out_s),
        'ssm_in_w': nrm((N_A, D_MODEL, IN_PROJ_DIM), D_MODEL),
        'ssm_conv_w': nrm((N_A, CONV_W, CONV_DIM), CONV_W),
        'ssm_conv_b': 0.02 * jax.random.normal(nk(), (N_A, CONV_DIM), f32),
        'ssm_dt_bias': dt + jnp.log(-jnp.expm1(-dt)),
        'ssm_a_log': jnp.log(jax.random.uniform(nk(), (N_A, SSM_HEADS), f32, 1.0, 16.0)),
        'ssm_d': gain((N_A, SSM_HEADS)),
        'ssm_norm': gain((N_A, D_INNER)),
        'ssm_out_w': nrm((N_A, D_INNER, D_MODEL), D_INNER, out_s),
        'kv_in_norm': gain((D_MODEL,)),
        'w_dkv': nrm((D_MODEL, KV_LORA), D_MODEL),
        'ckv_norm': gain((KV_LORA,)),
        'w_kr': nrm((D_MODEL, QK_ROPE), D_MODEL),
        'w_uk': nrm((KV_LORA, MLA_HEADS * QK_NOPE), KV_LORA),
        'w_uv': nrm((KV_LORA, MLA_HEADS * V_HEAD), KV_LORA),
        'k_nope_norm': gain((QK_NOPE,)),
        'k_rope_norm': gain((QK_ROPE,)),
        'w_dq': nrm((N_B, D_MODEL, Q_LORA), D_MODEL),
        'cq_norm': gain((N_B, Q_LORA)),
        'w_uq': nrm((N_B, Q_LORA, MLA_HEADS * QK_HEAD), Q_LORA),
        'q_nope_norm': gain((N_B, QK_NOPE)),
        'q_rope_norm': gain((N_B, QK_ROPE)),
        'mla_out_w': nrm((N_B, MLA_HEADS * V_HEAD, D_MODEL), MLA_HEADS * V_HEAD, out_s),
    }


def reference(x, p, positions, ln_mix, ln_mlp, mlp_w1, mlp_w2, ple_norm, ple_gate_w, ple_proj_w,
              ssm_in_w, ssm_conv_w, ssm_conv_b, ssm_dt_bias, ssm_a_log, ssm_d, ssm_norm, ssm_out_w,
              kv_in_norm, w_dkv, ckv_norm, w_kr, w_uk, w_uv, k_nope_norm, k_rope_norm,
              w_dq, cq_norm, w_uq, q_nope_norm, q_rope_norm, mla_out_w):
    h = x
    shared = None
    for i in range(DEPTH):
        hn = rmsnorm(h, ln_mix[i])
        if i < N_A:
            j = i
            h = h + mamba2_mixer(hn, ssm_in_w[j], ssm_conv_w[j], ssm_conv_b[j], ssm_dt_bias[j],
                                 ssm_a_log[j], ssm_d[j], ssm_norm[j], ssm_out_w[j])
        else:
            j = i - N_A
            if shared is None:
                shared = mla_shared_kv(h, kv_in_norm, w_dkv, ckv_norm, w_kr, w_uk, w_uv,
                                       k_nope_norm, k_rope_norm, positions)
            k_nope, k_rope, v = shared
            h = h + mla_mixer(hn, w_dq[j], cq_norm[j], w_uq[j], q_nope_norm[j], q_rope_norm[j],
                              mla_out_w[j], k_nope, k_rope, v, positions)
        h = h + sq_relu_mlp(rmsnorm(h, ln_mlp[i]), mlp_w1[i], mlp_w2[i])
        h = per_layer_input(h, p[i], ple_norm[i], ple_gate_w[i], ple_proj_w[i])
    return h
```

```python
import functools
import math
from typing import NamedTuple

import jax
import jax.numpy as jnp
from jax import lax
from jax.experimental import pallas as pl
from jax.experimental.pallas import tpu as pltpu

F32 = jnp.float32
BF16 = jnp.bfloat16
EPS = 1e-6

LANES = 128
SSD_CHUNK = 128
HEADDIM = 64
SSM_STATE = 128
CONV_W = 4
CONV_HIST = 8
MASK_CHUNK_SHIFT = 6
QK_NOPE = 128
QK_ROPE = 64
V_HEAD = 128
QK_PAD = 256
ROPE_THETA = 10000.0
MIB = 1024 * 1024


class _Cfg(NamedTuple):
    batch: int
    seq: int
    d_model: int
    d_ff: int
    ple_dim: int
    d_inner: int
    ssm_groups: int
    mla_heads: int
    q_lora: int
    kv_lora: int


class _Tiles(NamedTuple):
    tm_proj: int
    tn_proj: int
    tm_mlp: int
    tf_mlp: int
    tm_ple: int
    tm_mla: int
    t_attn: int
    vmem_mb: int


def _cparams(sem, vmem_mb):
    return pltpu.CompilerParams(dimension_semantics=sem, vmem_limit_bytes=vmem_mb * MIB)


def _rms(x, g):
    return x * lax.rsqrt(jnp.mean(x * x, axis=-1, keepdims=True) + EPS) * g


def _sigmoid(x):
    return 1.0 / (1.0 + jnp.exp(-x))


def _softplus(x):
    return jnp.maximum(x, 0.0) + jnp.log1p(jnp.exp(-jnp.abs(x)))


def _dot(a, b):
    return jnp.dot(a, b, preferred_element_type=F32)


def _dot_nt(a, b):
    return lax.dot_general(a, b, (((1,), (1,)), ((), ())), preferred_element_type=F32)


def _split3(v):
    hi = v.astype(BF16)
    r = v - hi.astype(F32)
    mid = r.astype(BF16)
    lo = (r - mid.astype(F32)).astype(BF16)
    return hi, mid, lo


def _inproj_kernel(h_ref, g_ref, w_ref, wdt_ref, zx_ref, dt_ref, hn_ref):
    @pl.when(pl.program_id(1) == 0)
    def _():
        hn = _rms(h_ref[...], g_ref[...]).astype(BF16)
        hn_ref[...] = hn
        dt_ref[...] = _dot(hn, wdt_ref[...])

    zx_ref[...] = _dot(hn_ref[...], w_ref[...]).astype(zx_ref.dtype)


def _inproj(h, gain, w_zx, w_dt, t):
    m, k = h.shape
    n = w_zx.shape[1]
    tm, tn = t.tm_proj, t.tn_proj
    return pl.pallas_call(
        _inproj_kernel,
        grid=(m // tm, n // tn),
        in_specs=[
            pl.BlockSpec((tm, k), lambda i, j: (i, 0)),
            pl.BlockSpec((1, k), lambda i, j: (0, 0)),
            pl.BlockSpec((k, tn), lambda i, j: (0, j)),
            pl.BlockSpec((k, LANES), lambda i, j: (0, 0)),
        ],
        out_specs=[
            pl.BlockSpec((tm, tn), lambda i, j: (i, j)),
            pl.BlockSpec((tm, LANES), lambda i, j: (i, 0)),
        ],
        out_shape=[jax.ShapeDtypeStruct((m, n), F32), jax.ShapeDtypeStruct((m, LANES), F32)],
        scratch_shapes=[pltpu.VMEM((tm, k), BF16)],
        compiler_params=_cparams(("parallel", "arbitrary"), t.vmem_mb),
        name="ssm_in_proj",
    )(h, gain, w_zx, w_dt)


def _ssd_kernel(z_ref, x_ref, b_ref, c_ref, dt_ref, cw_ref, cb_ref, dtb_ref, alog_ref, dfull_ref, ng_ref,
                out_ref, buf_ref, xc_ref, state_ref, *, d_inner, gn, groups):
    ln = SSD_CHUNK
    cdim = d_inner + 2 * gn
    heads = d_inner // HEADDIM
    hg = heads // groups
    gw = hg * HEADDIM
    c = pl.program_id(1)

    @pl.when(c == 0)
    def _():
        buf_ref[0:CONV_HIST, :] = jnp.zeros((CONV_HIST, cdim), F32)
        state_ref[...] = jnp.zeros_like(state_ref)

    @pl.when(c > 0)
    def _():
        buf_ref[0:CONV_HIST, :] = buf_ref[ln:ln + CONV_HIST, :]

    buf_ref[CONV_HIST:CONV_HIST + ln, 0:d_inner] = x_ref[...]
    buf_ref[CONV_HIST:CONV_HIST + ln, d_inner:d_inner + gn] = b_ref[...]
    buf_ref[CONV_HIST:CONV_HIST + ln, d_inner + gn:cdim] = c_ref[...]

    cwid = 512 if cdim % 512 == 0 else LANES
    for j in range(cdim // cwid):
        cols = slice(j * cwid, (j + 1) * cwid)
        acc = cb_ref[:, cols]
        for k in range(CONV_W):
            r0 = CONV_HIST - (CONV_W - 1) + k
            acc = acc + cw_ref[k:k + 1, cols] * buf_ref[r0:r0 + ln, cols]
        xc_ref[:, cols] = acc * _sigmoid(acc)

    dt = _softplus(dt_ref[...] + dtb_ref[...])
    a = dt * (-jnp.exp(alog_ref[...]))
    row = lax.broadcasted_iota(jnp.int32, (ln, ln), 0)
    col = lax.broadcasted_iota(jnp.int32, (ln, ln), 1)
    causal = row >= col
    tril = jnp.where(causal, 1.0, 0.0).astype(BF16)
    a_hi, a_mid, a_lo = _split3(a)
    acs = _dot(tril, a_hi) + _dot(tril, a_mid) + _dot(tril, a_lo)
    acs_t = acs.T
    dt_t = dt.T
    lo_half = lax.broadcasted_iota(jnp.int32, (ln, LANES), 1) < HEADDIM
    lo_half_s = lax.broadcasted_iota(jnp.int32, (SSM_STATE, LANES), 1) < HEADDIM

    for g in range(groups):
        bg = xc_ref[:, d_inner + g * SSM_STATE:d_inner + (g + 1) * SSM_STATE]
        cg = xc_ref[:, d_inner + gn + g * SSM_STATE:d_inner + gn + (g + 1) * SSM_STATE]
        cb = _dot_nt(cg.astype(BF16), bg.astype(BF16))
        bt = bg.T
        ys = []
        for p in range(hg // 2):
            pr = (g * hg) // 2 + p
            pcols = slice(pr * LANES, (pr + 1) * LANES)
            m_parts, s_parts, c_parts, cds = [], [], [], []
            for hh in (2 * pr, 2 * pr + 1):
                a_row = acs_t[hh:hh + 1, :]
                dt_row = dt_t[hh:hh + 1, :]
                a_col = jnp.broadcast_to(acs[:, hh:hh + 1], (ln, ln))
                dec = jnp.exp(jnp.where(causal, a_col - a_row, -jnp.inf))
                m_parts.append((cb * dec * dt_row).astype(BF16))
                last = a_col[ln - 1:ln, :]
                w = jnp.exp(last - a_row) * dt_row
                s_parts.append((bt * w).astype(BF16))
                c_parts.append((cg * jnp.exp(a_col)).astype(BF16))
                cds.append(jnp.exp(last))
            x2 = xc_ref[:, pcols]
            xa = jnp.where(lo_half, x2, 0.0).astype(BF16)
            xb = jnp.where(lo_half, 0.0, x2).astype(BF16)
            st = state_ref[:, pcols]
            sta = jnp.where(lo_half_s, st, 0.0).astype(BF16)
            stb = jnp.where(lo_half_s, 0.0, st).astype(BF16)
            lhs = jnp.concatenate(m_parts + c_parts, axis=1)
            rhs = jnp.concatenate([xa, xb, sta, stb], axis=0)
            ys.append(_dot(lhs, rhs))
            s_new = _dot(jnp.concatenate(s_parts, axis=1), jnp.concatenate([xa, xb], axis=0))
            cd = jnp.where(lo_half_s[0:1, :], cds[0], cds[1])
            state_ref[:, pcols] = st * cd + s_new
        gcols = slice(g * gw, (g + 1) * gw)
        yg = jnp.concatenate(ys, axis=1) + dfull_ref[:, gcols] * xc_ref[:, gcols]
        zg = z_ref[:, gcols]
        yg = yg * (zg * _sigmoid(zg))
        out_ref[:, gcols] = _rms(yg, ng_ref[:, gcols]).astype(out_ref.dtype)


def _ssd(zx, dt_raw, conv_w, conv_b, dt_bias, a_log, d_full, norm_g, cfg, t):
    ln = SSD_CHUNK
    d_inner = cfg.d_inner
    gn = cfg.ssm_groups * SSM_STATE
    cdim = d_inner + 2 * gn
    nc = cfg.seq // ln
    m = cfg.batch * cfg.seq
    assert d_inner % gn == 0 or gn % d_inner == 0
    b_blk = (2 * d_inner) // gn
    row = lambda b, c: b * nc + c
    kern = functools.partial(_ssd_kernel, d_inner=d_inner, gn=gn, groups=cfg.ssm_groups)
    const = lambda b, c: (0, 0)
    return pl.pallas_call(
        kern,
        grid=(cfg.batch, nc),
        in_specs=[
            pl.BlockSpec((ln, d_inner), lambda b, c: (row(b, c), 0)),
            pl.BlockSpec((ln, d_inner), lambda b, c: (row(b, c), 1)),
            pl.BlockSpec((ln, gn), lambda b, c: (row(b, c), b_blk)),
            pl.BlockSpec((ln, gn), lambda b, c: (row(b, c), b_blk + 1)),
            pl.BlockSpec((ln, LANES), lambda b, c: (row(b, c), 0)),
            pl.BlockSpec((CONV_W, cdim), const),
            pl.BlockSpec((1, cdim), const),
            pl.BlockSpec((1, LANES), const),
            pl.BlockSpec((1, LANES), const),
            pl.BlockSpec((1, d_inner), const),
            pl.BlockSpec((1, d_inner), const),
        ],
        out_specs=pl.BlockSpec((ln, d_inner), lambda b, c: (row(b, c), 0)),
        out_shape=jax.ShapeDtypeStruct((m, d_inner), BF16),
        scratch_shapes=[
            pltpu.VMEM((CONV_HIST + ln, cdim), F32),
            pltpu.VMEM((ln, cdim), F32),
            pltpu.VMEM((SSM_STATE, d_inner), F32),
        ],
        compiler_params=_cparams(("parallel", "arbitrary"), t.vmem_mb),
        name="ssm_scan",
    )(zx, zx, zx, zx, dt_raw, conv_w, conv_b, dt_bias, a_log, d_full, norm_g)


def _matmul_res_kernel(a_ref, w_ref, r_ref, o_ref):
    o_ref[...] = r_ref[...] + _dot(a_ref[...], w_ref[...])


def _matmul_res(a, w, res, t, name):
    m, k = a.shape
    n = w.shape[1]
    tm, tn = t.tm_proj, min(t.tn_proj, n)
    return pl.pallas_call(
        _matmul_res_kernel,
        grid=(m // tm, n // tn),
        in_specs=[
            pl.BlockSpec((tm, k), lambda i, j: (i, 0)),
            pl.BlockSpec((k, tn), lambda i, j: (0, j)),
            pl.BlockSpec((tm, tn), lambda i, j: (i, j)),
        ],
        out_specs=pl.BlockSpec((tm, tn), lambda i, j: (i, j)),
        out_shape=jax.ShapeDtypeStruct((m, n), F32),
        compiler_params=_cparams(("parallel", "parallel"), t.vmem_mb),
        name=name,
    )(a, w, res)


def _mlp_kernel(h_ref, g_ref, w1_ref, w2_ref, o_ref, hn_ref):
    @pl.when(pl.program_id(1) == 0)
    def _():
        h = h_ref[...]
        hn_ref[...] = _rms(h, g_ref[...]).astype(BF16)
        o_ref[...] = h

    a = jnp.maximum(_dot(hn_ref[...], w1_ref[...]), 0.0)
    o_ref[...] += _dot((a * a).astype(BF16), w2_ref[...])


def _mlp(h, gain, w1, w2, t):
    m, d = h.shape
    f = w1.shape[1]
    tm, tf = t.tm_mlp, t.tf_mlp
    return pl.pallas_call(
        _mlp_kernel,
        grid=(m // tm, f // tf),
        in_specs=[
            pl.BlockSpec((tm, d), lambda i, j: (i, 0)),
            pl.BlockSpec((1, d), lambda i, j: (0, 0)),
            pl.BlockSpec((d, tf), lambda i, j: (0, j)),
            pl.BlockSpec((tf, d), lambda i, j: (j, 0)),
        ],
        out_specs=pl.BlockSpec((tm, d), lambda i, j: (i, 0)),
        out_shape=jax.ShapeDtypeStruct((m, d), F32),
        scratch_shapes=[pltpu.VMEM((tm, d), BF16)],
        compiler_params=_cparams(("parallel", "arbitrary"), t.vmem_mb),
        name="sq_relu_mlp",
    )(h, gain, w1, w2)


def _ple_kernel(h_ref, g_ref, wg_ref, p_ref, wp_ref, o_ref, *, tn):
    hn = _rms(h_ref[...], g_ref[...]).astype(BF16)
    pb = p_ref[...].astype(BF16)
    for j in range(h_ref.shape[1] // tn):
        cols = slice(j * tn, (j + 1) * tn)
        gate = _sigmoid(_dot(hn, wg_ref[:, cols]))
        o_ref[:, cols] = h_ref[:, cols] + gate * _dot(pb, wp_ref[:, cols])


def _ple(h, gain, wg, p, wp, t):
    m, d = h.shape
    pd = p.shape[1]
    tm = t.tm_ple
    tn = 512 if d % 512 == 0 else LANES
    return pl.pallas_call(
        functools.partial(_ple_kernel, tn=tn),
        grid=(m // tm,),
        in_specs=[
            pl.BlockSpec((tm, d), lambda i: (i, 0)),
            pl.BlockSpec((1, d), lambda i: (0, 0)),
            pl.BlockSpec((d, d), lambda i: (0, 0)),
            pl.BlockSpec((tm, pd), lambda i: (i, 0)),
            pl.BlockSpec((pd, d), lambda i: (0, 0)),
        ],
        out_specs=pl.BlockSpec((tm, d), lambda i: (i, 0)),
        out_shape=jax.ShapeDtypeStruct((m, d), F32),
        compiler_params=_cparams(("parallel",), t.vmem_mb),
        name="per_layer_input",
    )(h, gain, wg, p, wp)


def _rope(x, cos_f, sin_s):
    lane = lax.broadcasted_iota(jnp.int32, x.shape, 1)
    half = QK_ROPE // 2
    rot = jnp.where(lane < half, pltpu.roll(x, LANES - half, 1), pltpu.roll(x, half, 1))
    return x * cos_f + rot * sin_s


def _rms_rope_part(x, g_pad):
    ms = jnp.sum(x * x, axis=-1, keepdims=True) * (1.0 / QK_ROPE)
    return x * lax.rsqrt(ms + EPS) * g_pad


def _mla_a_kernel(h_ref, gkv_ref, gq_ref, wdkv_ref, wkr_ref, wdq_ref, ckvn_ref, krn_ref, cqn_ref,
                  pos_ref, fr_ref, sg_ref, ckv_ref, cq_ref, kr_ref, cos_ref, sin_ref):
    h = h_ref[...]
    y = h * lax.rsqrt(jnp.mean(h * h, axis=-1, keepdims=True) + EPS)
    hkv = (y * gkv_ref[...]).astype(BF16)
    hq = (y * gq_ref[...]).astype(BF16)
    ckv_ref[...] = _rms(_dot(hkv, wdkv_ref[...]), ckvn_ref[...]).astype(BF16)
    cq_ref[...] = _rms(_dot(hq, wdq_ref[...]), cqn_ref[...]).astype(BF16)
    ang = pos_ref[...].astype(F32) * fr_ref[...]
    sg = sg_ref[...]
    cos_f = jnp.cos(ang) * (sg * sg)
    sin_s = jnp.sin(ang) * sg
    cos_ref[...] = cos_f
    sin_ref[...] = sin_s
    kr = _rms_rope_part(_dot(hkv, wkr_ref[...]), krn_ref[...])
    kr_ref[...] = _rope(kr, cos_f, sin_s).astype(BF16)


def _mla_a(h, g_kv, g_q, w_dkv, w_kr, w_dq, ckv_n, kr_n, cq_n, pos, freqs, sign, t):
    m, d = h.shape
    kvl = w_dkv.shape[1]
    ql = w_dq.shape[1]
    tm = t.tm_mla
    rowblk = lambda w: pl.BlockSpec((tm, w), lambda i: (i, 0))
    full = lambda a: pl.BlockSpec(a.shape, lambda i: (0, 0))
    return pl.pallas_call(
        _mla_a_kernel,
        grid=(m // tm,),
        in_specs=[rowblk(d), full(g_kv), full(g_q), full(w_dkv), full(w_kr), full(w_dq), full(ckv_n),
                  full(kr_n), full(cq_n), rowblk(1), full(freqs), full(sign)],
        out_specs=[rowblk(kvl), rowblk(ql), rowblk(LANES), rowblk(LANES), rowblk(LANES)],
        out_shape=[jax.ShapeDtypeStruct((m, kvl), BF16), jax.ShapeDtypeStruct((m, ql), BF16),
                   jax.ShapeDtypeStruct((m, LANES), BF16), jax.ShapeDtypeStruct((m, LANES), F32),
                   jax.ShapeDtypeStruct((m, LANES), F32)],
        compiler_params=_cparams(("parallel",), t.vmem_mb),
        name="mla_down_proj",
    )(h, g_kv, g_q, w_dkv, w_kr, w_dq, ckv_n, kr_n, cq_n, pos, freqs, sign)


def _mla_b_kernel(ckv_ref, cq_ref, kr_ref, cos_ref, sin_ref, wuk_ref, wuvt_ref, wuq_ref, knn_ref, qnn_ref,
                  qrn_ref, q_ref, k_ref, vt_ref, *, scale):
    ckv = ckv_ref[...]
    kn = _rms(_dot(ckv, wuk_ref[...]), knn_ref[...])
    k_ref[0, 0, :, 0:QK_NOPE] = kn.astype(BF16)
    k_ref[0, 0, :, QK_NOPE:QK_PAD] = kr_ref[...]
    vt_ref[0, 0, 0] = _dot_nt(wuvt_ref[...], ckv).astype(BF16)
    q = _dot(cq_ref[...], wuq_ref[...])
    qn = _rms(q[:, 0:QK_NOPE], qnn_ref[...]) * scale
    qr = _rms_rope_part(q[:, QK_NOPE:QK_PAD], qrn_ref[...])
    qr = _rope(qr, cos_ref[...], sin_ref[...]) * scale
    q_ref[0, 0, :, 0:QK_NOPE] = qn.astype(BF16)
    q_ref[0, 0, :, QK_NOPE:QK_PAD] = qr.astype(BF16)


def _mla_b(ckv, cq, kr, cos_f, sin_s, w_uk, w_uvt, w_uq, kn_n, qn_n, qr_n, cfg, t):
    m, kvl = ckv.shape
    ql = cq.shape[1]
    hds = cfg.mla_heads
    tm = t.t_attn
    ns = cfg.seq // tm
    scale = (QK_NOPE + QK_ROPE) ** -0.5
    rowblk = lambda w: pl.BlockSpec((tm, w), lambda i, h: (i, 0))
    vec = pl.BlockSpec((1, LANES), lambda i, h: (0, 0))
    return pl.pallas_call(
        functools.partial(_mla_b_kernel, scale=scale),
        grid=(m // tm, hds),
        in_specs=[rowblk(kvl), rowblk(ql), rowblk(LANES), rowblk(LANES), rowblk(LANES),
                  pl.BlockSpec((kvl, QK_NOPE), lambda i, h: (0, h)),
                  pl.BlockSpec((V_HEAD, kvl), lambda i, h: (h, 0)),
                  pl.BlockSpec((ql, QK_PAD), lambda i, h: (0, h)),
                  vec, vec, vec],
        out_specs=[
            pl.BlockSpec((1, 1, tm, QK_PAD), lambda i, h: (i // ns, h, i % ns, 0)),
            pl.BlockSpec((1, 1, tm, QK_PAD), lambda i, h: (i // ns, h, i % ns, 0)),
            pl.BlockSpec((1, 1, 1, V_HEAD, tm), lambda i, h: (i // ns, h, i % ns, 0, 0)),
        ],
        out_shape=[jax.ShapeDtypeStruct((cfg.batch, hds, cfg.seq, QK_PAD), BF16),
                   jax.ShapeDtypeStruct((cfg.batch, hds, cfg.seq, QK_PAD), BF16),
                   jax.ShapeDtypeStruct((cfg.batch, hds, ns, V_HEAD, tm), BF16)],
        compiler_params=_cparams(("parallel", "arbitrary"), t.vmem_mb),
        name="mla_up_proj",
    )(ckv, cq, kr, cos_f, sin_s, w_uk, w_uvt, w_uq, kn_n, qn_n, qr_n)


def _attn_kernel(q_ref, k_ref, vt_ref, o_ref, m_ref, l_ref, acc_ref, *, tq):
    qi = pl.program_id(2)
    q = q_ref[0, 0]
    m_ref[...] = jnp.full_like(m_ref, -jnp.inf)
    l_ref[...] = jnp.zeros_like(l_ref)
    acc_ref[...] = jnp.zeros_like(acc_ref)

    def step(kj, masked):
        k = k_ref[0, 0, pl.ds(pl.multiple_of(kj * tq, tq), tq), :]
        s = _dot_nt(k, q)
        if masked:
            krow = kj * tq + lax.broadcasted_iota(jnp.int32, (tq, tq), 0)
            qcol = qi * tq + lax.broadcasted_iota(jnp.int32, (tq, tq), 1)
            visible = (krow >> MASK_CHUNK_SHIFT) <= (qcol >> MASK_CHUNK_SHIFT)
            s = jnp.where(visible, s, -jnp.inf)
        m_prev = m_ref[...]
        m_new = jnp.maximum(m_prev, jnp.max(s, axis=0, keepdims=True))
        p = jnp.exp(s - m_new)
        alpha = jnp.exp(m_prev - m_new)
        l_ref[...] = alpha * l_ref[...] + jnp.sum(p, axis=0, keepdims=True)
        acc_ref[...] = acc_ref[...] * alpha + _dot(vt_ref[0, 0, kj], p.astype(BF16))
        m_ref[...] = m_new

    def body(kj, carry):
        step(kj, False)
        return carry

    lax.fori_loop(0, qi, body, 0)
    step(qi, True)
    o = acc_ref[...] / l_ref[...]
    o_ref[0] = o.T.astype(o_ref.dtype)


def _attn(q, k, vt, cfg, t):
    tq = t.t_attn
    hds = cfg.mla_heads
    nq = cfg.seq // tq
    return pl.pallas_call(
        functools.partial(_attn_kernel, tq=tq),
        grid=(cfg.batch, hds, nq),
        in_specs=[
            pl.BlockSpec((1, 1, tq, QK_PAD), lambda b, h, i: (b, h, i, 0)),
            pl.BlockSpec((1, 1, cfg.seq, QK_PAD), lambda b, h, i: (b, h, 0, 0)),
            pl.BlockSpec((1, 1, nq, V_HEAD, tq), lambda b, h, i: (b, h, 0, 0, 0)),
        ],
        out_specs=pl.BlockSpec((1, tq, V_HEAD), lambda b, h, i: (b, i, h)),
        out_shape=jax.ShapeDtypeStruct((cfg.batch, cfg.seq, hds * V_HEAD), BF16),
        scratch_shapes=[pltpu.VMEM((1, tq), F32), pltpu.VMEM((1, tq), F32), pltpu.VMEM((V_HEAD, tq), F32)],
        compiler_params=_cparams(("parallel", "parallel", "arbitrary"), t.vmem_mb),
        name="mla_flash_attn",
    )(q, k, vt)


def _pad_lanes(v, width=LANES):
    v = v.reshape(1, -1)
    return jnp.pad(v, ((0, 0), (0, width - v.shape[1])))


def _forward(cfg, t, x, p, positions, ln_mix, ln_mlp, mlp_w1, mlp_w2, ple_norm, ple_gate_w, ple_proj_w,
             ssm_in_w, ssm_conv_w, ssm_conv_b, ssm_dt_bias, ssm_a_log, ssm_d, ssm_norm, ssm_out_w,
             kv_in_norm, w_dkv, ckv_norm, w_kr, w_uk, w_uv, k_nope_norm, k_rope_norm,
             w_dq, cq_norm, w_uq, q_nope_norm, q_rope_norm, mla_out_w):
    depth = ln_mix.shape[0]
    n_a = ssm_in_w.shape[0]
    m = cfg.batch * cfg.seq
    d = cfg.d_model
    hds = cfg.mla_heads
    row = lambda v: v.reshape(1, -1).astype(F32)
    bf = lambda w: w.astype(BF16)
    h = x.reshape(m, d)

    kv_ready = False
    for i in range(depth):
        if i < n_a:
            j = i
            d_inner = cfg.d_inner
            cdim = d_inner + 2 * cfg.ssm_groups * SSM_STATE
            w_in = ssm_in_w[j]
            w_zx = bf(w_in[:, :d_inner + cdim])
            w_dt = bf(jnp.pad(w_in[:, d_inner + cdim:], ((0, 0), (0, LANES - d_inner // HEADDIM))))
            zx, dt_raw = _inproj(h, row(ln_mix[i]), w_zx, w_dt, t)
            yn = _ssd(zx, dt_raw, ssm_conv_w[j], row(ssm_conv_b[j]), _pad_lanes(ssm_dt_bias[j]),
                      _pad_lanes(ssm_a_log[j]), row(jnp.repeat(ssm_d[j], HEADDIM)), row(ssm_norm[j]), cfg, t)
            h = _matmul_res(yn, bf(ssm_out_w[j]), h, t, "ssm_out_proj")
        else:
            j = i - n_a
            if not kv_ready:
                kv_h = h
                kv_ready = True
            half = QK_ROPE // 2
            fr = ROPE_THETA ** (-jnp.arange(half, dtype=F32) / half)
            freqs = _pad_lanes(jnp.concatenate([fr, fr]))
            sign = _pad_lanes(jnp.concatenate([-jnp.ones((half,), F32), jnp.ones((half,), F32)]))
            w_kr_p = bf(jnp.pad(w_kr, ((0, 0), (0, LANES - QK_ROPE))))
            assert depth - n_a == 1
            ckv, cq, kr, cos_f, sin_s = _mla_a(
                kv_h, row(kv_in_norm), row(ln_mix[i]), bf(w_dkv), w_kr_p, bf(w_dq[j]), row(ckv_norm),
                _pad_lanes(k_rope_norm), row(cq_norm[j]), positions.reshape(m, 1), freqs, sign, t)
            wq = w_uq[j].reshape(cfg.q_lora, hds, QK_NOPE + QK_ROPE)
            wq = jnp.pad(wq, ((0, 0), (0, 0), (0, QK_PAD - QK_NOPE - QK_ROPE))).reshape(cfg.q_lora, hds * QK_PAD)
            q, k, vt = _mla_b(ckv, cq, kr, cos_f, sin_s, bf(w_uk), bf(w_uv.T), bf(wq), row(k_nope_norm),
                              row(q_nope_norm[j]), _pad_lanes(q_rope_norm[j]), cfg, t)
            o = _attn(q, k, vt, cfg, t)
            h = _matmul_res(o.reshape(m, hds * V_HEAD), bf(mla_out_w[j]), h, t, "mla_out_proj")
        h = _mlp(h, row(ln_mlp[i]), bf(mlp_w1[i]), bf(mlp_w2[i]), t)
        h = _ple(h, row(ple_norm[i]), bf(ple_gate_w[i]), p[i].reshape(m, cfg.ple_dim), bf(ple_proj_w[i]), t)
    return h.reshape(cfg.batch, cfg.seq, d)


_CFG = _Cfg(batch=2, seq=4096, d_model=2048, d_ff=8192, ple_dim=256, d_inner=4096, ssm_groups=8,
            mla_heads=16, q_lora=512, kv_lora=512)
_TILES = _Tiles(tm_proj=1024, tn_proj=512, tm_mlp=512, tf_mlp=512, tm_ple=512, tm_mla=512, t_attn=512,
                vmem_mb=48)


def kernel(x, p, positions, ln_mix, ln_mlp, mlp_w1, mlp_w2, ple_norm, ple_gate_w, ple_proj_w, ssm_in_w, ssm_conv_w, ssm_conv_b, ssm_dt_bias, ssm_a_log, ssm_d, ssm_norm, ssm_out_w, kv_in_norm, w_dkv, ckv_norm, w_kr, w_uk, w_uv, k_nope_norm, k_rope_norm, w_dq, cq_norm, w_uq, q_nope_norm, q_rope_norm, mla_out_w):
    return _forward(_CFG, _TILES, x, p, positions, ln_mix, ln_mlp, mlp_w1, mlp_w2, ple_norm, ple_gate_w,
                    ple_proj_w, ssm_in_w, ssm_conv_w, ssm_conv_b, ssm_dt_bias, ssm_a_log, ssm_d, ssm_norm,
                    ssm_out_w, kv_in_norm, w_dkv, ckv_norm, w_kr, w_uk, w_uv, k_nope_norm, k_rope_norm,
                    w_dq, cq_norm, w_uq, q_nope_norm, q_rope_norm, mla_out_w)
```

```python
import functools
from typing import NamedTuple

import jax
import jax.numpy as jnp
from jax import lax
from jax.experimental import pallas as pl
from jax.experimental.pallas import tpu as pltpu

F32 = jnp.float32
BF16 = jnp.bfloat16
EPS = 1e-6

LANES = 128
SUBLANES = 8
SSD_CHUNK = 128
HEADDIM = 64
SSM_STATE = 128
CONV_W = 4
CONV_HIST = 8
CONV_STRIDE = 4
MASK_CHUNK_SHIFT = 6
QK_NOPE = 128
QK_ROPE = 64
V_HEAD = 128
QK_PAD = 256
ROPE_THETA = 10000.0
MIB = 1024 * 1024


class _Cfg(NamedTuple):
    batch: int
    seq: int
    d_model: int
    d_ff: int
    ple_dim: int
    d_inner: int
    ssm_groups: int
    mla_heads: int
    q_lora: int
    kv_lora: int


class _Tiles(NamedTuple):
    tm_proj: int
    tn_proj: int
    tm_mlp: int
    tf_mlp: int
    tm_ple: int
    tm_mla: int
    t_attn: int
    attn_heads: int
    vmem_mb: int


def _cparams(sem, vmem_mb):
    return pltpu.CompilerParams(dimension_semantics=sem, vmem_limit_bytes=vmem_mb * MIB)


def _rms(x, g):
    return x * lax.rsqrt(jnp.mean(x * x, axis=-1, keepdims=True) + EPS) * g


def _sigmoid(x):
    return 1.0 / (1.0 + jnp.exp(-x))


def _softplus(x):
    return jnp.maximum(x, 0.0) + jnp.log1p(jnp.exp(-jnp.abs(x)))


def _dot(a, b):
    return jnp.dot(a, b, preferred_element_type=F32)


def _dot_nt(a, b):
    return lax.dot_general(a, b, (((1,), (1,)), ((), ())), preferred_element_type=F32)


def _split3(v):
    hi = v.astype(BF16)
    r = v - hi.astype(F32)
    mid = r.astype(BF16)
    lo = (r - mid.astype(F32)).astype(BF16)
    return hi, mid, lo


def _inproj_kernel(h_ref, g_ref, w_ref, wdt_ref, zx_ref, dt_ref, hn_ref):
    @pl.when(pl.program_id(1) == 0)
    def _():
        hn = _rms(h_ref[...], g_ref[...]).astype(BF16)
        hn_ref[...] = hn
        dt_ref[...] = _dot(hn, wdt_ref[...])

    zx_ref[...] = _dot(hn_ref[...], w_ref[...].astype(BF16)).astype(zx_ref.dtype)


def _inproj(h, gain, w_in, layer, n, w_dt, t):
    m, k = h.shape
    tm, tn = t.tm_proj, t.tn_proj
    return pl.pallas_call(
        _inproj_kernel,
        grid=(m // tm, n // tn),
        in_specs=[
            pl.BlockSpec((tm, k), lambda i, j: (i, 0), pipeline_mode=pl.Buffered(1)),
            pl.BlockSpec((1, k), lambda i, j: (0, 0)),
            pl.BlockSpec((None, k, tn), lambda i, j: (layer, 0, j)),
            pl.BlockSpec((k, LANES), lambda i, j: (0, 0)),
        ],
        out_specs=[
            pl.BlockSpec((tm, tn), lambda i, j: (i, j)),
            pl.BlockSpec((tm, LANES), lambda i, j: (i, 0)),
        ],
        out_shape=[jax.ShapeDtypeStruct((m, n), BF16), jax.ShapeDtypeStruct((m, LANES), F32)],
        scratch_shapes=[pltpu.VMEM((tm, k), BF16)],
        compiler_params=_cparams(("parallel", "arbitrary"), t.vmem_mb),
        name="ssm_in_proj",
    )(h, gain, w_in, w_dt)


def _ssd_kernel(z_ref, x_ref, b_ref, c_ref, dt_ref, cwb_ref, dtb_ref, alog_ref, dfull_ref, ng_ref,
                out_ref, buf_ref, xc_ref, state_ref, *, d_inner, gn, groups):
    ln = SSD_CHUNK
    cdim = d_inner + 2 * gn
    nslab = cdim // LANES
    sl_b = d_inner // LANES
    sl_c = (d_inner + gn) // LANES
    heads = d_inner // HEADDIM
    hg = heads // groups
    gw = hg * HEADDIM
    c = pl.program_id(1)

    @pl.when(c == 0)
    def _():
        buf_ref[:, 0:CONV_HIST, :] = jnp.zeros((nslab, CONV_HIST, LANES), F32)
        state_ref[...] = jnp.zeros_like(state_ref)

    @pl.when(c > 0)
    def _():
        buf_ref[:, 0:CONV_HIST, :] = buf_ref[:, ln:ln + CONV_HIST, :]

    for s in range(nslab):
        lo = s * LANES
        if lo < d_inner:
            src = x_ref[:, lo:lo + LANES]
        elif lo < d_inner + gn:
            src = b_ref[:, lo - d_inner:lo - d_inner + LANES]
        else:
            src = c_ref[:, lo - d_inner - gn:lo - d_inner - gn + LANES]
        buf_ref[s, CONV_HIST:CONV_HIST + ln, :] = src.astype(F32)

    rows_per_blk = SUBLANES * CONV_STRIDE

    def conv_slab(s, carry):
        wb = cwb_ref[s]
        for blk in range(ln // rows_per_blk):
            for i in range(CONV_STRIDE):
                t0 = blk * rows_per_blk + i
                acc = wb[CONV_W:CONV_W + 1, :]
                for k in range(CONV_W):
                    start = CONV_HIST + t0 - (CONV_W - 1) + k
                    acc = acc + wb[k:k + 1, :] * buf_ref[s, pl.ds(start, SUBLANES, stride=CONV_STRIDE), :]
                xc_ref[s, pl.ds(t0, SUBLANES, stride=CONV_STRIDE), :] = acc * _sigmoid(acc)
        return carry

    lax.fori_loop(0, nslab, conv_slab, 0, unroll=2)

    dt = _softplus(dt_ref[...] + dtb_ref[...])
    a = dt * (-jnp.exp(alog_ref[...]))
    row = lax.broadcasted_iota(jnp.int32, (ln, ln), 0)
    col = lax.broadcasted_iota(jnp.int32, (ln, ln), 1)
    causal = row >= col
    tril = jnp.where(causal, 1.0, 0.0).astype(BF16)
    a_hi, a_mid, a_lo = _split3(a)
    acs = _dot(tril, a_hi) + _dot(tril, a_mid) + _dot(tril, a_lo)
    acs_t = acs.T
    dt_t = dt.T
    lo_half = lax.broadcasted_iota(jnp.int32, (ln, LANES), 1) < HEADDIM
    lo_half_s = lax.broadcasted_iota(jnp.int32, (SSM_STATE, LANES), 1) < HEADDIM

    for g in range(groups):
        bg = xc_ref[sl_b + g]
        cg = xc_ref[sl_c + g]
        cb = _dot_nt(cg.astype(BF16), bg.astype(BF16))
        bt = bg.T
        ys = []
        for p in range(hg // 2):
            pr = (g * hg) // 2 + p
            pcols = slice(pr * LANES, (pr + 1) * LANES)
            m_parts, s_parts, c_parts, cds = [], [], [], []
            for hh in (2 * pr, 2 * pr + 1):
                a_row = acs_t[hh:hh + 1, :]
                dt_row = dt_t[hh:hh + 1, :]
                a_col = jnp.broadcast_to(acs[:, hh:hh + 1], (ln, ln))
                dec = jnp.exp(jnp.where(causal, a_col - a_row, -jnp.inf))
                m_parts.append((cb * dec * dt_row).astype(BF16))
                last = a_col[ln - 1:ln, :]
                w = jnp.exp(last - a_row) * dt_row
                s_parts.append((bt * w).astype(BF16))
                c_parts.append((cg * jnp.exp(a_col)).astype(BF16))
                cds.append(jnp.exp(last))
            x2 = xc_ref[pr]
            xa = jnp.where(lo_half, x2, 0.0).astype(BF16)
            xb = jnp.where(lo_half, 0.0, x2).astype(BF16)
            st = state_ref[:, pcols]
            sta = jnp.where(lo_half_s, st, 0.0).astype(BF16)
            stb = jnp.where(lo_half_s, 0.0, st).astype(BF16)
            lhs = jnp.concatenate(m_parts + c_parts, axis=1)
            rhs = jnp.concatenate([xa, xb, sta, stb], axis=0)
            y2 = _dot(lhs, rhs)
            ys.append(y2 + dfull_ref[:, pcols] * x2)
            s_new = _dot(jnp.concatenate(s_parts, axis=1), jnp.concatenate([xa, xb], axis=0))
            cd = jnp.where(lo_half_s[0:1, :], cds[0], cds[1])
            state_ref[:, pcols] = st * cd + s_new
        gcols = slice(g * gw, (g + 1) * gw)
        zg = z_ref[:, gcols].astype(F32)
        yg = jnp.concatenate(ys, axis=1) * (zg * _sigmoid(zg))
        out_ref[:, gcols] = _rms(yg, ng_ref[:, gcols]).astype(out_ref.dtype)


def _ssd(zx, dt_raw, conv_wb, dt_bias, a_log, d_full, norm_g, cfg, t):
    ln = SSD_CHUNK
    d_inner = cfg.d_inner
    gn = cfg.ssm_groups * SSM_STATE
    cdim = d_inner + 2 * gn
    nslab = cdim // LANES
    nc = cfg.seq // ln
    m = cfg.batch * cfg.seq
    b_blk = (2 * d_inner) // gn
    row = lambda b, c: b * nc + c
    kern = functools.partial(_ssd_kernel, d_inner=d_inner, gn=gn, groups=cfg.ssm_groups)
    const = lambda b, c: (0, 0)
    return pl.pallas_call(
        kern,
        grid=(cfg.batch, nc),
        in_specs=[
            pl.BlockSpec((ln, d_inner), lambda b, c: (row(b, c), 0)),
            pl.BlockSpec((ln, d_inner), lambda b, c: (row(b, c), 1)),
            pl.BlockSpec((ln, gn), lambda b, c: (row(b, c), b_blk)),
            pl.BlockSpec((ln, gn), lambda b, c: (row(b, c), b_blk + 1)),
            pl.BlockSpec((ln, LANES), lambda b, c: (row(b, c), 0)),
            pl.BlockSpec((nslab, SUBLANES, LANES), lambda b, c: (0, 0, 0)),
            pl.BlockSpec((1, LANES), const),
            pl.BlockSpec((1, LANES), const),
            pl.BlockSpec((1, d_inner), const),
            pl.BlockSpec((1, d_inner), const),
        ],
        out_specs=pl.BlockSpec((ln, d_inner), lambda b, c: (row(b, c), 0)),
        out_shape=jax.ShapeDtypeStruct((m, d_inner), BF16),
        scratch_shapes=[
            pltpu.VMEM((nslab, CONV_HIST + ln, LANES), F32),
            pltpu.VMEM((nslab, ln, LANES), F32),
            pltpu.VMEM((SSM_STATE, d_inner), F32),
        ],
        compiler_params=_cparams(("parallel", "arbitrary"), t.vmem_mb),
        name="ssm_scan",
    )(zx, zx, zx, zx, dt_raw, conv_wb, dt_bias, a_log, d_full, norm_g)


def _matmul_res_kernel(a_ref, w_ref, r_ref, o_ref):
    o_ref[...] = r_ref[...] + _dot(a_ref[...], w_ref[...].astype(BF16))


def _matmul_res(a, w, layer, res, t, name):
    m, k = a.shape
    n = w.shape[2]
    tm, tn = t.tm_proj, min(t.tn_proj, n)
    return pl.pallas_call(
        _matmul_res_kernel,
        grid=(m // tm, n // tn),
        in_specs=[
            pl.BlockSpec((tm, k), lambda i, j: (i, 0)),
            pl.BlockSpec((None, k, tn), lambda i, j: (layer, 0, j)),
            pl.BlockSpec((tm, tn), lambda i, j: (i, j)),
        ],
        out_specs=pl.BlockSpec((tm, tn), lambda i, j: (i, j)),
        out_shape=jax.ShapeDtypeStruct((m, n), F32),
        compiler_params=_cparams(("parallel", "parallel"), t.vmem_mb),
        name=name,
    )(a, w, res)


def _mlp_kernel(h_ref, g_ref, w1_ref, w2_ref, o_ref, hn_ref):
    @pl.when(pl.program_id(1) == 0)
    def _():
        h = h_ref[...]
        hn_ref[...] = _rms(h, g_ref[...]).astype(BF16)
        o_ref[...] = h

    a = jnp.maximum(_dot(hn_ref[...], w1_ref[...].astype(BF16)), 0.0)
    o_ref[...] += _dot((a * a).astype(BF16), w2_ref[...].astype(BF16))


def _mlp(h, gain, w1, w2, layer, t):
    m, d = h.shape
    f = w1.shape[2]
    tm, tf = t.tm_mlp, t.tf_mlp
    return pl.pallas_call(
        _mlp_kernel,
        grid=(m // tm, f // tf),
        in_specs=[
            pl.BlockSpec((tm, d), lambda i, j: (i, 0), pipeline_mode=pl.Buffered(1)),
            pl.BlockSpec((1, d), lambda i, j: (0, 0)),
            pl.BlockSpec((None, d, tf), lambda i, j: (layer, 0, j)),
            pl.BlockSpec((None, tf, d), lambda i, j: (layer, j, 0)),
        ],
        out_specs=pl.BlockSpec((tm, d), lambda i, j: (i, 0)),
        out_shape=jax.ShapeDtypeStruct((m, d), F32),
        scratch_shapes=[pltpu.VMEM((tm, d), BF16)],
        compiler_params=_cparams(("parallel", "arbitrary"), t.vmem_mb),
        name="sq_relu_mlp",
    )(h, gain, w1, w2)


def _ple_kernel(h_ref, g_ref, wg_ref, p_ref, wp_ref, o_ref, *, tn):
    hn = _rms(h_ref[...], g_ref[...]).astype(BF16)
    pb = p_ref[...].astype(BF16)
    for j in range(h_ref.shape[1] // tn):
        cols = slice(j * tn, (j + 1) * tn)
        gate = _sigmoid(_dot(hn, wg_ref[:, cols]))
        o_ref[:, cols] = h_ref[:, cols] + gate * _dot(pb, wp_ref[:, cols])


def _ple(h, gain, wg, p, layer, wp, t):
    m, d = h.shape
    pd = p.shape[2]
    tm = t.tm_ple
    tn = 512 if d % 512 == 0 else LANES
    return pl.pallas_call(
        functools.partial(_ple_kernel, tn=tn),
        grid=(m // tm,),
        in_specs=[
            pl.BlockSpec((tm, d), lambda i: (i, 0)),
            pl.BlockSpec((1, d), lambda i: (0, 0)),
            pl.BlockSpec((d, d), lambda i: (0, 0)),
            pl.BlockSpec((None, tm, pd), lambda i: (layer, i, 0)),
            pl.BlockSpec((pd, d), lambda i: (0, 0)),
        ],
        out_specs=pl.BlockSpec((tm, d), lambda i: (i, 0)),
        out_shape=jax.ShapeDtypeStruct((m, d), F32),
        compiler_params=_cparams(("parallel",), t.vmem_mb),
        name="per_layer_input",
    )(h, gain, wg, p, wp)


def _rope(x, cos_f, sin_s):
    lane = lax.broadcasted_iota(jnp.int32, x.shape, 1)
    half = QK_ROPE // 2
    rot = jnp.where(lane < half, pltpu.roll(x, LANES - half, 1), pltpu.roll(x, half, 1))
    return x * cos_f + rot * sin_s


def _rms_rope_part(x, g_pad):
    ms = jnp.sum(x * x, axis=-1, keepdims=True) * (1.0 / QK_ROPE)
    return x * lax.rsqrt(ms + EPS) * g_pad


def _mla_a_kernel(h_ref, gkv_ref, gq_ref, wdkv_ref, wkr_ref, wdq_ref, ckvn_ref, krn_ref, cqn_ref,
                  pos_ref, fr_ref, sg_ref, ckv_ref, cq_ref, kr_ref, cos_ref, sin_ref):
    h = h_ref[...]
    y = h * lax.rsqrt(jnp.mean(h * h, axis=-1, keepdims=True) + EPS)
    hkv = (y * gkv_ref[...]).astype(BF16)
    hq = (y * gq_ref[...]).astype(BF16)
    ckv_ref[...] = _rms(_dot(hkv, wdkv_ref[...]), ckvn_ref[...]).astype(BF16)
    cq_ref[...] = _rms(_dot(hq, wdq_ref[...]), cqn_ref[...]).astype(BF16)
    ang = pos_ref[...].astype(F32) * fr_ref[...]
    sg = sg_ref[...]
    cos_f = jnp.cos(ang) * (sg * sg)
    sin_s = jnp.sin(ang) * sg
    cos_ref[...] = cos_f
    sin_ref[...] = sin_s
    kr = _rms_rope_part(_dot(hkv, wkr_ref[...]), krn_ref[...])
    kr_ref[...] = _rope(kr, cos_f, sin_s).astype(BF16)


def _mla_a(h, g_kv, g_q, w_dkv, w_kr, w_dq, ckv_n, kr_n, cq_n, pos, freqs, sign, t):
    m, d = h.shape
    kvl = w_dkv.shape[1]
    ql = w_dq.shape[1]
    tm = t.tm_mla
    rowblk = lambda w: pl.BlockSpec((tm, w), lambda i: (i, 0))
    full = lambda a: pl.BlockSpec(a.shape, lambda i: (0, 0))
    return pl.pallas_call(
        _mla_a_kernel,
        grid=(m // tm,),
        in_specs=[rowblk(d), full(g_kv), full(g_q), full(w_dkv), full(w_kr), full(w_dq), full(ckv_n),
                  full(kr_n), full(cq_n), rowblk(1), full(freqs), full(sign)],
        out_specs=[rowblk(kvl), rowblk(ql), rowblk(LANES), rowblk(LANES), rowblk(LANES)],
        out_shape=[jax.ShapeDtypeStruct((m, kvl), BF16), jax.ShapeDtypeStruct((m, ql), BF16),
                   jax.ShapeDtypeStruct((m, LANES), BF16), jax.ShapeDtypeStruct((m, LANES), F32),
                   jax.ShapeDtypeStruct((m, LANES), F32)],
        compiler_params=_cparams(("parallel",), t.vmem_mb),
        name="mla_down_proj",
    )(h, g_kv, g_q, w_dkv, w_kr, w_dq, ckv_n, kr_n, cq_n, pos, freqs, sign)


def _mla_b_kernel(ckv_ref, cq_ref, kr_ref, cos_ref, sin_ref, wuk_ref, wuvt_ref, wuq_ref, knn_ref, qnn_ref,
                  qrn_ref, q_ref, k_ref, vt_ref, *, scale, heads):
    ckv = ckv_ref[...]
    cq = cq_ref[...]
    kr = kr_ref[...]
    cos_f = cos_ref[...]
    sin_s = sin_ref[...]
    for h in range(heads):
        kn = _rms(_dot(ckv, wuk_ref[:, h * QK_NOPE:(h + 1) * QK_NOPE]), knn_ref[...])
        k_ref[0, h, :, 0:QK_NOPE] = kn.astype(BF16)
        k_ref[0, h, :, QK_NOPE:QK_PAD] = kr
        vt_ref[0, h, 0] = _dot_nt(wuvt_ref[h * V_HEAD:(h + 1) * V_HEAD, :], ckv).astype(BF16)
        q = _dot(cq, wuq_ref[:, h * QK_PAD:(h + 1) * QK_PAD])
        qn = _rms(q[:, 0:QK_NOPE], qnn_ref[...]) * scale
        qr = _rms_rope_part(q[:, QK_NOPE:QK_PAD], qrn_ref[...])
        qr = _rope(qr, cos_f, sin_s) * scale
        q_ref[0, h, :, 0:QK_NOPE] = qn.astype(BF16)
        q_ref[0, h, :, QK_NOPE:QK_PAD] = qr.astype(BF16)


def _mla_b(ckv, cq, kr, cos_f, sin_s, w_uk, w_uvt, w_uq, kn_n, qn_n, qr_n, cfg, t):
    m, kvl = ckv.shape
    ql = cq.shape[1]
    hds = cfg.mla_heads
    tm = t.t_attn
    ns = cfg.seq // tm
    scale = (QK_NOPE + QK_ROPE) ** -0.5
    rowblk = lambda w: pl.BlockSpec((tm, w), lambda i: (i, 0))
    full = lambda a: pl.BlockSpec(a.shape, lambda i: (0, 0))
    return pl.pallas_call(
        functools.partial(_mla_b_kernel, scale=scale, heads=hds),
        grid=(m // tm,),
        in_specs=[rowblk(kvl), rowblk(ql), rowblk(LANES), rowblk(LANES), rowblk(LANES),
                  full(w_uk), full(w_uvt), full(w_uq), full(kn_n), full(qn_n), full(qr_n)],
        out_specs=[
            pl.BlockSpec((1, hds, tm, QK_PAD), lambda i: (i // ns, 0, i % ns, 0)),
            pl.BlockSpec((1, hds, tm, QK_PAD), lambda i: (i // ns, 0, i % ns, 0)),
            pl.BlockSpec((1, hds, 1, V_HEAD, tm), lambda i: (i // ns, 0, i % ns, 0, 0)),
        ],
        out_shape=[jax.ShapeDtypeStruct((cfg.batch, hds, cfg.seq, QK_PAD), BF16),
                   jax.ShapeDtypeStruct((cfg.batch, hds, cfg.seq, QK_PAD), BF16),
                   jax.ShapeDtypeStruct((cfg.batch, hds, ns, V_HEAD, tm), BF16)],
        compiler_params=_cparams(("parallel",), t.vmem_mb),
        name="mla_up_proj",
    )(ckv, cq, kr, cos_f, sin_s, w_uk, w_uvt, w_uq, kn_n, qn_n, qr_n)


def _attn_kernel(q_ref, k_ref, vt_ref, o_ref, m_ref, l_ref, acc_ref, sa_ref, sb_ref, bma_ref, bmb_ref, *, tq, hb):
    qi = pl.program_id(2)
    tk = tq // 2
    m_ref[...] = jnp.full_like(m_ref, -jnp.inf)
    l_ref[...] = jnp.zeros_like(l_ref)
    acc_ref[...] = jnp.zeros_like(acc_ref)

    def visible_mask(k0):
        krow = k0 + lax.broadcasted_iota(jnp.int32, (tk, tq), 0)
        qcol = qi * tq + lax.broadcasted_iota(jnp.int32, (tk, tq), 1)
        return (krow >> MASK_CHUNK_SHIFT) <= (qcol >> MASK_CHUNK_SHIFT)

    def qk_stage(j, half, s_ref, bm_ref, masked):
        k0 = j * tq + half * tk
        if masked:
            visible = visible_mask(k0)
        for hh in range(hb):
            k = k_ref[0, hh, pl.ds(pl.multiple_of(k0, tk), tk), :]
            s = _dot_nt(k, q_ref[0, hh])
            if masked:
                s = jnp.where(visible, s, -jnp.inf)
            s_ref[hh] = s
            bm_ref[hh] = jnp.max(s, axis=0, keepdims=True)

    def sm_stage(j, half, s_ref, bm_ref, masked=False):
        if masked:
            visible = visible_mask(j * tq + half * tk)
        for hh in range(hb):
            m_prev = m_ref[hh]
            if masked:
                s = jnp.where(visible, s_ref[hh], -jnp.inf)
                m_new = jnp.maximum(m_prev, jnp.max(s, axis=0, keepdims=True))
            else:
                s = s_ref[hh]
                m_new = jnp.maximum(m_prev, bm_ref[hh])
            p = jnp.exp(s - m_new)
            alpha = jnp.exp(m_prev - m_new)
            l_ref[hh] = alpha * l_ref[hh] + jnp.sum(p, axis=0, keepdims=True)
            vt = vt_ref[0, hh, j, :, half * tk:(half + 1) * tk]
            acc_ref[hh] = acc_ref[hh] * alpha + _dot(vt, p.astype(BF16))
            m_ref[hh] = m_new

    qk_stage(0, 0, sa_ref, bma_ref, False)

    def body(j, carry):
        qk_stage(j, 1, sb_ref, bmb_ref, False)
        sm_stage(j, 0, sa_ref, bma_ref)
        qk_stage(j + 1, 0, sa_ref, bma_ref, False)
        sm_stage(j, 1, sb_ref, bmb_ref)
        return carry

    lax.fori_loop(0, qi, body, 0)
    qk_stage(qi, 1, sb_ref, bmb_ref, True)
    sm_stage(qi, 0, sa_ref, bma_ref, masked=True)
    sm_stage(qi, 1, sb_ref, bmb_ref)
    for hh in range(hb):
        o = acc_ref[hh] / l_ref[hh]
        o_ref[0, :, hh * V_HEAD:(hh + 1) * V_HEAD] = o.T.astype(o_ref.dtype)


def _attn(q, k, vt, cfg, t):
    tq = t.t_attn
    hb = t.attn_heads
    hds = cfg.mla_heads
    nq = cfg.seq // tq
    return pl.pallas_call(
        functools.partial(_attn_kernel, tq=tq, hb=hb),
        grid=(cfg.batch, hds // hb, nq),
        in_specs=[
            pl.BlockSpec((1, hb, tq, QK_PAD), lambda b, h, i: (b, h, i, 0)),
            pl.BlockSpec((1, hb, cfg.seq, QK_PAD), lambda b, h, i: (b, h, 0, 0)),
            pl.BlockSpec((1, hb, nq, V_HEAD, tq), lambda b, h, i: (b, h, 0, 0, 0)),
        ],
        out_specs=pl.BlockSpec((1, tq, hb * V_HEAD), lambda b, h, i: (b, i, h)),
        out_shape=jax.ShapeDtypeStruct((cfg.batch, cfg.seq, hds * V_HEAD), BF16),
        scratch_shapes=[pltpu.VMEM((hb, 1, tq), F32), pltpu.VMEM((hb, 1, tq), F32),
                        pltpu.VMEM((hb, V_HEAD, tq), F32),
                        pltpu.VMEM((hb, tq // 2, tq), F32), pltpu.VMEM((hb, tq // 2, tq), F32),
                        pltpu.VMEM((hb, 1, tq), F32), pltpu.VMEM((hb, 1, tq), F32)],
        compiler_params=_cparams(("parallel", "parallel", "arbitrary"), t.vmem_mb),
        name="mla_flash_attn",
    )(q, k, vt)


def _pad_lanes(v, width=LANES):
    v = v.reshape(1, -1)
    return jnp.pad(v, ((0, 0), (0, width - v.shape[1])))


def _forward(cfg, t, x, p, positions, ln_mix, ln_mlp, mlp_w1, mlp_w2, ple_norm, ple_gate_w, ple_proj_w,
             ssm_in_w, ssm_conv_w, ssm_conv_b, ssm_dt_bias, ssm_a_log, ssm_d, ssm_norm, ssm_out_w,
             kv_in_norm, w_dkv, ckv_norm, w_kr, w_uk, w_uv, k_nope_norm, k_rope_norm,
             w_dq, cq_norm, w_uq, q_nope_norm, q_rope_norm, mla_out_w):
    depth = ln_mix.shape[0]
    n_a = ssm_in_w.shape[0]
    m = cfg.batch * cfg.seq
    d = cfg.d_model
    hds = cfg.mla_heads
    row = lambda v: v.reshape(1, -1).astype(F32)
    bf = lambda w: w.astype(BF16)
    h = x.reshape(m, d)
    p2 = p.reshape(depth, m, cfg.ple_dim)

    kv_ready = False
    for i in range(depth):
        if i < n_a:
            j = i
            d_inner = cfg.d_inner
            cdim = d_inner + 2 * cfg.ssm_groups * SSM_STATE
            w_dt = bf(jnp.pad(ssm_in_w[j][:, d_inner + cdim:], ((0, 0), (0, LANES - d_inner // HEADDIM))))
            zx, dt_raw = _inproj(h, row(ln_mix[i]), ssm_in_w, j, d_inner + cdim, w_dt, t)
            cwb = jnp.concatenate([ssm_conv_w[j], ssm_conv_b[j][None, :],
                                   jnp.zeros((SUBLANES - CONV_W - 1, cdim), F32)], axis=0)
            cwb = cwb.reshape(SUBLANES, cdim // LANES, LANES).transpose(1, 0, 2)
            yn = _ssd(zx, dt_raw, cwb, _pad_lanes(ssm_dt_bias[j]), _pad_lanes(ssm_a_log[j]),
                      row(jnp.repeat(ssm_d[j], HEADDIM)), row(ssm_norm[j]), cfg, t)
            h = _matmul_res(yn, ssm_out_w, j, h, t, "ssm_out_proj")
        else:
            j = i - n_a
            if not kv_ready:
                kv_h = h
                kv_ready = True
            half = QK_ROPE // 2
            fr = ROPE_THETA ** (-jnp.arange(half, dtype=F32) / half)
            freqs = _pad_lanes(jnp.concatenate([fr, fr]))
            sign = _pad_lanes(jnp.concatenate([-jnp.ones((half,), F32), jnp.ones((half,), F32)]))
            w_kr_p = bf(jnp.pad(w_kr, ((0, 0), (0, LANES - QK_ROPE))))
            assert depth - n_a == 1
            ckv, cq, kr, cos_f, sin_s = _mla_a(
                kv_h, row(kv_in_norm), row(ln_mix[i]), bf(w_dkv), w_kr_p, bf(w_dq[j]), row(ckv_norm),
                _pad_lanes(k_rope_norm), row(cq_norm[j]), positions.reshape(m, 1), freqs, sign, t)
            wq = w_uq[j].reshape(cfg.q_lora, hds, QK_NOPE + QK_ROPE)
            wq = jnp.pad(wq, ((0, 0), (0, 0), (0, QK_PAD - QK_NOPE - QK_ROPE))).reshape(cfg.q_lora, hds * QK_PAD)
            q, k, vt = _mla_b(ckv, cq, kr, cos_f, sin_s, bf(w_uk), bf(w_uv.T), bf(wq), row(k_nope_norm),
                              row(q_nope_norm[j]), _pad_lanes(q_rope_norm[j]), cfg, t)
            o = _attn(q, k, vt, cfg, t)
            h = _matmul_res(o.reshape(m, hds * V_HEAD), bf(mla_out_w), j, h, t, "mla_out_proj")
        h = _mlp(h, row(ln_mlp[i]), mlp_w1, mlp_w2, i, t)
        h = _ple(h, row(ple_norm[i]), bf(ple_gate_w[i]), p2, i, bf(ple_proj_w[i]), t)
    return h.reshape(cfg.batch, cfg.seq, d)


_CFG = _Cfg(batch=2, seq=4096, d_model=2048, d_ff=8192, ple_dim=256, d_inner=4096, ssm_groups=8,
            mla_heads=16, q_lora=512, kv_lora=512)
_TILES = _Tiles(tm_proj=1024, tn_proj=512, tm_mlp=1024, tf_mlp=512, tm_ple=512, tm_mla=512, t_attn=512,
                attn_heads=4, vmem_mb=56)


def kernel(x, p, positions, ln_mix, ln_mlp, mlp_w1, mlp_w2, ple_norm, ple_gate_w, ple_proj_w, ssm_in_w, ssm_conv_w, ssm_conv_b, ssm_dt_bias, ssm_a_log, ssm_d, ssm_norm, ssm_out_w, kv_in_norm, w_dkv, ckv_norm, w_kr, w_uk, w_uv, k_nope_norm, k_rope_norm, w_dq, cq_norm, w_uq, q_nope_norm, q_rope_norm, mla_out_w):
    return _forward(_CFG, _TILES, x, p, positions, ln_mix, ln_mlp, mlp_w1, mlp_w2, ple_norm, ple_gate_w,
                    ple_proj_w, ssm_in_w, ssm_conv_w, ssm_conv_b, ssm_dt_bias, ssm_a_log, ssm_d, ssm_norm,
                    ssm_out_w, kv_in_norm, w_dkv, ckv_norm, w_kr, w_uk, w_uv, k_nope_norm, k_rope_norm,
                    w_dq, cq_norm, w_uq, q_nope_norm, q_rope_norm, mla_out_w)
```

```python
import functools
import math
from typing import NamedTuple

import jax
import jax.numpy as jnp
from jax import lax
from jax.experimental import pallas as pl
from jax.experimental.pallas import tpu as pltpu

F32 = jnp.float32
BF16 = jnp.bfloat16
EPS = 1e-6

LANES = 128
SUBLANES = 8
SSD_CHUNK = 128
HEADDIM = 64
SSM_STATE = 128
CONV_W = 4
CONV_HIST = 8
CONV_STRIDE = 4
MASK_CHUNK_SHIFT = 6
QK_NOPE = 128
QK_ROPE = 64
V_HEAD = 128
QK_PAD = 256
ROPE_THETA = 10000.0
LOG2E = math.log2(math.e)
MIB = 1024 * 1024


class _Cfg(NamedTuple):
    batch: int
    seq: int
    d_model: int
    d_ff: int
    ple_dim: int
    d_inner: int
    ssm_groups: int
    mla_heads: int
    q_lora: int
    kv_lora: int


class _Tiles(NamedTuple):
    tm_proj: int
    tn_proj: int
    tn_in: int
    tm_mlp: int
    tf_mlp: int
    tm_ple: int
    tm_mla: int
    t_attn: int
    attn_heads: int
    vmem_mb: int


def _cparams(sem, vmem_mb):
    return pltpu.CompilerParams(dimension_semantics=sem, vmem_limit_bytes=vmem_mb * MIB)


def _rms(x, g):
    return x * lax.rsqrt(jnp.mean(x * x, axis=-1, keepdims=True) + EPS) * g


def _sigmoid(x):
    return 1.0 / (1.0 + jnp.exp2(x * (-LOG2E)))


def _softplus(x):
    return jnp.maximum(x, 0.0) + jnp.log1p(jnp.exp(-jnp.abs(x)))


def _dot(a, b):
    return jnp.dot(a, b, preferred_element_type=F32)


def _dot_nt(a, b):
    return lax.dot_general(a, b, (((1,), (1,)), ((), ())), preferred_element_type=F32)


def _split3(v):
    hi = v.astype(BF16)
    r = v - hi.astype(F32)
    mid = r.astype(BF16)
    lo = (r - mid.astype(F32)).astype(BF16)
    return hi, mid, lo


def _inproj_kernel(h_ref, g_ref, w_ref, wdt_ref, zx_ref, dt_ref, hn_ref):
    @pl.when(pl.program_id(1) == 0)
    def _():
        hn = _rms(h_ref[...], g_ref[...]).astype(BF16)
        hn_ref[...] = hn
        dt_ref[...] = _dot(hn, wdt_ref[...])

    zx_ref[...] = _dot(hn_ref[...], w_ref[...].astype(BF16)).astype(zx_ref.dtype)


def _inproj(h, gain, w_in, layer, n, w_dt, t):
    m, k = h.shape
    tm, tn = t.tm_proj, t.tn_in
    return pl.pallas_call(
        _inproj_kernel,
        grid=(m // tm, n // tn),
        in_specs=[
            pl.BlockSpec((tm, k), lambda i, j: (i, 0), pipeline_mode=pl.Buffered(1)),
            pl.BlockSpec((1, k), lambda i, j: (0, 0)),
            pl.BlockSpec((None, k, tn), lambda i, j: (layer, 0, j)),
            pl.BlockSpec((k, LANES), lambda i, j: (0, 0)),
        ],
        out_specs=[
            pl.BlockSpec((tm, tn), lambda i, j: (i, j)),
            pl.BlockSpec((tm, LANES), lambda i, j: (i, 0)),
        ],
        out_shape=[jax.ShapeDtypeStruct((m, n), BF16), jax.ShapeDtypeStruct((m, LANES), F32)],
        scratch_shapes=[pltpu.VMEM((tm, k), BF16)],
        compiler_params=_cparams(("parallel", "arbitrary"), t.vmem_mb),
        name="ssm_in_proj",
    )(h, gain, w_in, w_dt)


def _ssd_kernel(z_ref, x_ref, b_ref, c_ref, dt_ref, cwb_ref, dtb_ref, alog_ref, dfull_ref, ng_ref, e_ref,
                out_ref, buf_ref, xc_ref, state_ref, *, d_inner, gn, groups):
    ln = SSD_CHUNK
    cdim = d_inner + 2 * gn
    nslab = cdim // LANES
    sl_b = d_inner // LANES
    sl_c = (d_inner + gn) // LANES
    heads = d_inner // HEADDIM
    hg = heads // groups
    gw = hg * HEADDIM
    c = pl.program_id(1)

    @pl.when(c == 0)
    def _():
        buf_ref[:, 0:CONV_HIST, :] = jnp.zeros((nslab, CONV_HIST, LANES), F32)
        state_ref[...] = jnp.zeros_like(state_ref)

    @pl.when(c > 0)
    def _():
        buf_ref[:, 0:CONV_HIST, :] = buf_ref[:, ln:ln + CONV_HIST, :]

    for s in range(nslab):
        lo = s * LANES
        if lo < d_inner:
            src = x_ref[:, lo:lo + LANES]
        elif lo < d_inner + gn:
            src = b_ref[:, lo - d_inner:lo - d_inner + LANES]
        else:
            src = c_ref[:, lo - d_inner - gn:lo - d_inner - gn + LANES]
        buf_ref[s, CONV_HIST:CONV_HIST + ln, :] = src.astype(F32)

    rows_per_blk = SUBLANES * CONV_STRIDE

    def conv_slab(s, carry):
        wb = cwb_ref[s]
        for blk in range(ln // rows_per_blk):
            for i in range(CONV_STRIDE):
                t0 = blk * rows_per_blk + i
                acc = wb[CONV_W:CONV_W + 1, :]
                for k in range(CONV_W):
                    start = CONV_HIST + t0 - (CONV_W - 1) + k
                    acc = acc + wb[k:k + 1, :] * buf_ref[s, pl.ds(start, SUBLANES, stride=CONV_STRIDE), :]
                xc_ref[s, pl.ds(t0, SUBLANES, stride=CONV_STRIDE), :] = acc * _sigmoid(acc)
        return carry

    lax.fori_loop(0, nslab, conv_slab, 0, unroll=8)

    dt = _softplus(dt_ref[...] + dtb_ref[...])
    a = dt * (-jnp.exp(alog_ref[...])) * LOG2E
    row = lax.broadcasted_iota(jnp.int32, (ln, ln), 0)
    col = lax.broadcasted_iota(jnp.int32, (ln, ln), 1)
    causal = row >= col
    tril = jnp.where(causal, 1.0, 0.0).astype(BF16)
    a_hi, a_mid, a_lo = _split3(a)
    acs = _dot(tril, a_hi) + _dot(tril, a_mid) + _dot(tril, a_lo)
    acs_t = acs.T
    last = acs[ln - 1:ln, :]
    eacs = jnp.exp2(acs)
    wend = jnp.exp2(last - acs) * dt
    lo_half = lax.broadcasted_iota(jnp.int32, (ln, LANES), 1) < HEADDIM

    def hi_mid(v):
        hi = v.astype(BF16)
        return jnp.concatenate([hi, (v - hi.astype(F32)).astype(BF16)], axis=1)

    per_head = jnp.concatenate([hi_mid(eacs), hi_mid(wend), hi_mid(dt)], axis=0)

    for g in range(groups):
        gcols = slice(g * gw, (g + 1) * gw)
        bg = xc_ref[sl_b + g]
        cg = xc_ref[sl_c + g].astype(BF16)
        cb = _dot_nt(cg, bg.astype(BF16))
        bt = bg.T.astype(BF16)
        eg = e_ref[:, gcols]
        xg = jnp.concatenate([xc_ref[g * (gw // LANES) + q] for q in range(gw // LANES)], axis=1)
        st = state_ref[:, gcols]
        lanes_g = _dot(per_head, eg)
        eacs_g = lanes_g[0:ln]
        y_off = _dot(cg, st.astype(BF16)) * eacs_g
        s_new = _dot(bt, (xg * lanes_g[ln:2 * ln]).astype(BF16))
        state_ref[:, gcols] = st * eacs_g[ln - 1:ln, :] + s_new
        xdt = xg * lanes_g[2 * ln:3 * ln]
        ys = []
        for p in range(hg // 2):
            pr = (g * hg) // 2 + p
            m_parts = []
            for hh in (2 * pr, 2 * pr + 1):
                a_row = acs_t[hh:hh + 1, :]
                a_col = jnp.broadcast_to(acs[:, hh:hh + 1], (ln, ln))
                dec = jnp.exp2(jnp.where(causal, a_col - a_row, -jnp.inf))
                m_parts.append((cb * dec).astype(BF16))
            x2 = xdt[:, p * LANES:(p + 1) * LANES]
            xa = jnp.where(lo_half, x2, 0.0).astype(BF16)
            xb = jnp.where(lo_half, 0.0, x2).astype(BF16)
            ys.append(_dot(jnp.concatenate(m_parts, axis=1), jnp.concatenate([xa, xb], axis=0)))
        yg = jnp.concatenate(ys, axis=1) + y_off + dfull_ref[:, gcols] * xg
        zg = z_ref[:, gcols].astype(F32)
        yg = yg * (zg * _sigmoid(zg))
        out_ref[:, gcols] = _rms(yg, ng_ref[:, gcols]).astype(out_ref.dtype)


def _ssd(zx, dt_raw, conv_wb, dt_bias, a_log, d_full, norm_g, cfg, t):
    ln = SSD_CHUNK
    d_inner = cfg.d_inner
    gn = cfg.ssm_groups * SSM_STATE
    cdim = d_inner + 2 * gn
    nslab = cdim // LANES
    nc = cfg.seq // ln
    m = cfg.batch * cfg.seq
    b_blk = (2 * d_inner) // gn
    row = lambda b, c: b * nc + c
    kern = functools.partial(_ssd_kernel, d_inner=d_inner, gn=gn, groups=cfg.ssm_groups)
    head_lanes = (jnp.arange(d_inner)[None, :] // HEADDIM == jnp.arange(LANES)[:, None]).astype(BF16)
    head_lanes = jnp.concatenate([head_lanes, head_lanes], axis=0)
    const = lambda b, c: (0, 0)
    return pl.pallas_call(
        kern,
        grid=(cfg.batch, nc),
        in_specs=[
            pl.BlockSpec((ln, d_inner), lambda b, c: (row(b, c), 0)),
            pl.BlockSpec((ln, d_inner), lambda b, c: (row(b, c), 1)),
            pl.BlockSpec((ln, gn), lambda b, c: (row(b, c), b_blk)),
            pl.BlockSpec((ln, gn), lambda b, c: (row(b, c), b_blk + 1)),
            pl.BlockSpec((ln, LANES), lambda b, c: (row(b, c), 0)),
            pl.BlockSpec((nslab, SUBLANES, LANES), lambda b, c: (0, 0, 0)),
            pl.BlockSpec((1, LANES), const),
            pl.BlockSpec((1, LANES), const),
            pl.BlockSpec((1, d_inner), const),
            pl.BlockSpec((1, d_inner), const),
            pl.BlockSpec((2 * LANES, d_inner), const),
        ],
        out_specs=pl.BlockSpec((ln, d_inner), lambda b, c: (row(b, c), 0)),
        out_shape=jax.ShapeDtypeStruct((m, d_inner), BF16),
        scratch_shapes=[
            pltpu.VMEM((nslab, CONV_HIST + ln, LANES), F32),
            pltpu.VMEM((nslab, ln, LANES), F32),
            pltpu.VMEM((SSM_STATE, d_inner), F32),
        ],
        compiler_params=_cparams(("parallel", "arbitrary"), t.vmem_mb),
        name="ssm_scan",
    )(zx, zx, zx, zx, dt_raw, conv_wb, dt_bias, a_log, d_full, norm_g, head_lanes)


def _matmul_res_kernel(a_ref, w_ref, r_ref, o_ref):
    o_ref[...] = r_ref[...] + _dot(a_ref[...], w_ref[...].astype(BF16))


def _matmul_res(a, w, layer, res, t, name):
    m, k = a.shape
    n = w.shape[2]
    tm, tn = t.tm_proj, min(t.tn_proj, n)
    return pl.pallas_call(
        _matmul_res_kernel,
        grid=(m // tm, n // tn),
        in_specs=[
            pl.BlockSpec((tm, k), lambda i, j: (i, 0)),
            pl.BlockSpec((None, k, tn), lambda i, j: (layer, 0, j)),
            pl.BlockSpec((tm, tn), lambda i, j: (i, j)),
        ],
        out_specs=pl.BlockSpec((tm, tn), lambda i, j: (i, j)),
        out_shape=jax.ShapeDtypeStruct((m, n), F32),
        compiler_params=_cparams(("parallel", "parallel"), t.vmem_mb),
        name=name,
    )(a, w, res)


def _mlp_kernel(h_ref, g_ref, w1_ref, w2_ref, o_ref, hn_ref):
    @pl.when(pl.program_id(1) == 0)
    def _():
        h = h_ref[...]
        hn_ref[...] = _rms(h, g_ref[...]).astype(BF16)
        o_ref[...] = h

    a = jnp.maximum(_dot(hn_ref[...], w1_ref[...].astype(BF16)), 0.0)
    o_ref[...] += _dot((a * a).astype(BF16), w2_ref[...].astype(BF16))


def _mlp(h, gain, w1, w2, layer, t):
    m, d = h.shape
    f = w1.shape[2]
    tm, tf = t.tm_mlp, t.tf_mlp
    return pl.pallas_call(
        _mlp_kernel,
        grid=(m // tm, f // tf),
        in_specs=[
            pl.BlockSpec((tm, d), lambda i, j: (i, 0), pipeline_mode=pl.Buffered(1)),
            pl.BlockSpec((1, d), lambda i, j: (0, 0)),
            pl.BlockSpec((None, d, tf), lambda i, j: (layer, 0, j)),
            pl.BlockSpec((None, tf, d), lambda i, j: (layer, j, 0)),
        ],
        out_specs=pl.BlockSpec((tm, d), lambda i, j: (i, 0)),
        out_shape=jax.ShapeDtypeStruct((m, d), F32),
        scratch_shapes=[pltpu.VMEM((tm, d), BF16)],
        compiler_params=_cparams(("parallel", "arbitrary"), t.vmem_mb),
        name="sq_relu_mlp",
    )(h, gain, w1, w2)


def _ple_kernel(h_ref, g_ref, wg_ref, p_ref, wp_ref, o_ref, *, tn):
    hn = _rms(h_ref[...], g_ref[...]).astype(BF16)
    pb = p_ref[...].astype(BF16)
    for j in range(h_ref.shape[1] // tn):
        cols = slice(j * tn, (j + 1) * tn)
        gate = _sigmoid(_dot(hn, wg_ref[:, cols]))
        o_ref[:, cols] = h_ref[:, cols] + gate * _dot(pb, wp_ref[:, cols])


def _ple(h, gain, wg, p, layer, wp, t):
    m, d = h.shape
    pd = p.shape[2]
    tm = t.tm_ple
    tn = 512 if d % 512 == 0 else LANES
    return pl.pallas_call(
        functools.partial(_ple_kernel, tn=tn),
        grid=(m // tm,),
        in_specs=[
            pl.BlockSpec((tm, d), lambda i: (i, 0)),
            pl.BlockSpec((1, d), lambda i: (0, 0)),
            pl.BlockSpec((d, d), lambda i: (0, 0)),
            pl.BlockSpec((None, tm, pd), lambda i: (layer, i, 0)),
            pl.BlockSpec((pd, d), lambda i: (0, 0)),
        ],
        out_specs=pl.BlockSpec((tm, d), lambda i: (i, 0)),
        out_shape=jax.ShapeDtypeStruct((m, d), F32),
        compiler_params=_cparams(("parallel",), t.vmem_mb),
        name="per_layer_input",
    )(h, gain, wg, p, wp)


def _rope(x, cos_f, sin_s):
    lane = lax.broadcasted_iota(jnp.int32, x.shape, 1)
    half = QK_ROPE // 2
    rot = jnp.where(lane < half, pltpu.roll(x, LANES - half, 1), pltpu.roll(x, half, 1))
    return x * cos_f + rot * sin_s


def _rms_rope_part(x, g_pad):
    ms = jnp.sum(x * x, axis=-1, keepdims=True) * (1.0 / QK_ROPE)
    return x * lax.rsqrt(ms + EPS) * g_pad


def _mla_a_kernel(h_ref, gkv_ref, gq_ref, wdkv_ref, wkr_ref, wdq_ref, ckvn_ref, krn_ref, cqn_ref,
                  pos_ref, fr_ref, sg_ref, ckv_ref, cq_ref, kr_ref, cos_ref, sin_ref):
    h = h_ref[...]
    y = h * lax.rsqrt(jnp.mean(h * h, axis=-1, keepdims=True) + EPS)
    hkv = (y * gkv_ref[...]).astype(BF16)
    hq = (y * gq_ref[...]).astype(BF16)
    ckv_ref[...] = _rms(_dot(hkv, wdkv_ref[...]), ckvn_ref[...]).astype(BF16)
    cq_ref[...] = _rms(_dot(hq, wdq_ref[...]), cqn_ref[...]).astype(BF16)
    ang = pos_ref[...].astype(F32) * fr_ref[...]
    sg = sg_ref[...]
    cos_f = jnp.cos(ang) * (sg * sg)
    sin_s = jnp.sin(ang) * sg
    half = QK_ROPE // 2
    cos_ref[...] = cos_f.T[0:half]
    sin_ref[...] = sin_s.T[half:2 * half]
    kr = _rms_rope_part(_dot(hkv, wkr_ref[...]), krn_ref[...])
    kr_ref[...] = _rope(kr, cos_f, sin_s).astype(BF16)


def _mla_a(h, g_kv, g_q, w_dkv, w_kr, w_dq, ckv_n, kr_n, cq_n, pos, freqs, sign, t):
    m, d = h.shape
    kvl = w_dkv.shape[1]
    ql = w_dq.shape[1]
    tm = t.tm_mla
    rowblk = lambda w: pl.BlockSpec((tm, w), lambda i: (i, 0))
    full = lambda a: pl.BlockSpec(a.shape, lambda i: (0, 0))
    colblk = pl.BlockSpec((QK_ROPE // 2, tm), lambda i: (0, i))
    return pl.pallas_call(
        _mla_a_kernel,
        grid=(m // tm,),
        in_specs=[rowblk(d), full(g_kv), full(g_q), full(w_dkv), full(w_kr), full(w_dq), full(ckv_n),
                  full(kr_n), full(cq_n), rowblk(1), full(freqs), full(sign)],
        out_specs=[rowblk(kvl), rowblk(ql), rowblk(LANES), colblk, colblk],
        out_shape=[jax.ShapeDtypeStruct((m, kvl), BF16), jax.ShapeDtypeStruct((m, ql), BF16),
                   jax.ShapeDtypeStruct((m, LANES), BF16), jax.ShapeDtypeStruct((QK_ROPE // 2, m), F32),
                   jax.ShapeDtypeStruct((QK_ROPE // 2, m), F32)],
        compiler_params=_cparams(("parallel",), t.vmem_mb),
        name="mla_down_proj",
    )(h, g_kv, g_q, w_dkv, w_kr, w_dq, ckv_n, kr_n, cq_n, pos, freqs, sign)


def _mla_b_kernel(ckv_ref, cq_ref, kr_ref, cos_ref, sin_ref, wuk_ref, wuvt_ref, wuqt_ref, knn_ref, qnn_ref,
                  qrn_ref, q_ref, k_ref, vt_ref, *, scale, heads):
    ckv = ckv_ref[...]
    cq = cq_ref[...]
    kr = kr_ref[...]
    cos_t = cos_ref[...]
    sin_t = sin_ref[...]
    half = QK_ROPE // 2
    g_nope = qnn_ref[...] * scale
    g_rope = qrn_ref[...] * scale
    for h in range(heads):
        kn = _rms(_dot(ckv, wuk_ref[:, h * QK_NOPE:(h + 1) * QK_NOPE]), knn_ref[...])
        k_ref[0, h, :, 0:QK_NOPE] = kn.astype(BF16)
        k_ref[0, h, :, QK_NOPE:QK_PAD] = kr
        vt_ref[0, h, 0] = _dot_nt(wuvt_ref[h * V_HEAD:(h + 1) * V_HEAD, :], ckv).astype(BF16)
        qt = _dot_nt(wuqt_ref[h * QK_PAD:(h + 1) * QK_PAD, :], cq)
        qn = qt[0:QK_NOPE]
        qn = qn * lax.rsqrt(jnp.mean(qn * qn, axis=0, keepdims=True) + EPS) * g_nope
        qr = qt[QK_NOPE:QK_NOPE + QK_ROPE]
        qr = qr * lax.rsqrt(jnp.mean(qr * qr, axis=0, keepdims=True) + EPS) * g_rope
        x1 = qr[0:half]
        x2 = qr[half:QK_ROPE]
        q_ref[0, h, 0, 0:QK_NOPE, :] = qn.astype(BF16)
        q_ref[0, h, 0, QK_NOPE:QK_NOPE + half, :] = (x1 * cos_t - x2 * sin_t).astype(BF16)
        q_ref[0, h, 0, QK_NOPE + half:QK_NOPE + QK_ROPE, :] = (x2 * cos_t + x1 * sin_t).astype(BF16)
        q_ref[0, h, 0, QK_NOPE + QK_ROPE:QK_PAD, :] = jnp.zeros((QK_PAD - QK_NOPE - QK_ROPE, qt.shape[1]), BF16)


def _mla_b(ckv, cq, kr, cos_t, sin_t, w_uk, w_uvt, w_uqt, kn_n, qn_n, qr_n, cfg, t):
    m, kvl = ckv.shape
    ql = cq.shape[1]
    hds = cfg.mla_heads
    tm = t.t_attn
    ns = cfg.seq // tm
    scale = (QK_NOPE + QK_ROPE) ** -0.5 * LOG2E
    rowblk = lambda w: pl.BlockSpec((tm, w), lambda i: (i, 0))
    full = lambda a: pl.BlockSpec(a.shape, lambda i: (0, 0))
    colblk = pl.BlockSpec((QK_ROPE // 2, tm), lambda i: (0, i))
    return pl.pallas_call(
        functools.partial(_mla_b_kernel, scale=scale, heads=hds),
        grid=(m // tm,),
        in_specs=[rowblk(kvl), rowblk(ql), rowblk(LANES), colblk, colblk,
                  full(w_uk), full(w_uvt), full(w_uqt), full(kn_n), full(qn_n), full(qr_n)],
        out_specs=[
            pl.BlockSpec((1, hds, 1, QK_PAD, tm), lambda i: (i // ns, 0, i % ns, 0, 0)),
            pl.BlockSpec((1, hds, tm, QK_PAD), lambda i: (i // ns, 0, i % ns, 0)),
            pl.BlockSpec((1, hds, 1, V_HEAD, tm), lambda i: (i // ns, 0, i % ns, 0, 0)),
        ],
        out_shape=[jax.ShapeDtypeStruct((cfg.batch, hds, ns, QK_PAD, tm), BF16),
                   jax.ShapeDtypeStruct((cfg.batch, hds, cfg.seq, QK_PAD), BF16),
                   jax.ShapeDtypeStruct((cfg.batch, hds, ns, V_HEAD, tm), BF16)],
        compiler_params=_cparams(("parallel",), t.vmem_mb),
        name="mla_up_proj",
    )(ckv, cq, kr, cos_t, sin_t, w_uk, w_uvt, w_uqt, kn_n, qn_n, qr_n)


def _attn_kernel(q_ref, k_ref, vt_ref, o_ref, m_ref, l_ref, acc_ref, sa_ref, sb_ref, bma_ref, bmb_ref, *, tq, hb):
    qi = pl.program_id(2)
    tk = tq // 2
    m_ref[...] = jnp.full_like(m_ref, -jnp.inf)
    l_ref[...] = jnp.zeros_like(l_ref)
    acc_ref[...] = jnp.zeros_like(acc_ref)

    def visible_mask(k0):
        krow = k0 + lax.broadcasted_iota(jnp.int32, (tk, tq), 0)
        qcol = qi * tq + lax.broadcasted_iota(jnp.int32, (tk, tq), 1)
        return (krow >> MASK_CHUNK_SHIFT) <= (qcol >> MASK_CHUNK_SHIFT)

    def qk_stage(j, half, s_ref, bm_ref, masked):
        k0 = j * tq + half * tk
        if masked:
            visible = visible_mask(k0)
        for hh in range(hb):
            k = k_ref[0, hh, pl.ds(pl.multiple_of(k0, tk), tk), :]
            s = _dot(k, q_ref[0, hh, 0])
            if masked:
                s = jnp.where(visible, s, -jnp.inf)
            s_ref[hh] = s
            bm_ref[hh] = jnp.max(s, axis=0, keepdims=True)

    def sm_stage(j, half, s_ref, bm_ref, masked=False):
        if masked:
            visible = visible_mask(j * tq + half * tk)
        for hh in range(hb):
            m_prev = m_ref[hh]
            if masked:
                s = jnp.where(visible, s_ref[hh], -jnp.inf)
                m_new = jnp.maximum(m_prev, jnp.max(s, axis=0, keepdims=True))
            else:
                s = s_ref[hh]
                m_new = jnp.maximum(m_prev, bm_ref[hh])
            p = jnp.exp2(s - m_new)
            alpha = jnp.exp2(m_prev - m_new)
            l_ref[hh] = alpha * l_ref[hh] + jnp.sum(p, axis=0, keepdims=True)
            vt = vt_ref[0, hh, j, :, half * tk:(half + 1) * tk]
            acc_ref[hh] = acc_ref[hh] * alpha + _dot(vt, p.astype(BF16))
            m_ref[hh] = m_new

    qk_stage(0, 0, sa_ref, bma_ref, False)

    def body(j, carry):
        qk_stage(j, 1, sb_ref, bmb_ref, False)
        sm_stage(j, 0, sa_ref, bma_ref)
        qk_stage(j + 1, 0, sa_ref, bma_ref, False)
        sm_stage(j, 1, sb_ref, bmb_ref)
        return carry

    lax.fori_loop(0, qi, body, 0)
    qk_stage(qi, 1, sb_ref, bmb_ref, True)
    sm_stage(qi, 0, sa_ref, bma_ref, masked=True)
    sm_stage(qi, 1, sb_ref, bmb_ref)
    for hh in range(hb):
        o = acc_ref[hh] / l_ref[hh]
        o_ref[0, :, hh * V_HEAD:(hh + 1) * V_HEAD] = o.T.astype(o_ref.dtype)


def _attn(q, k, vt, cfg, t):
    tq = t.t_attn
    hb = t.attn_heads
    hds = cfg.mla_heads
    nq = cfg.seq // tq
    return pl.pallas_call(
        functools.partial(_attn_kernel, tq=tq, hb=hb),
        grid=(cfg.batch, hds // hb, nq),
        in_specs=[
            pl.BlockSpec((1, hb, 1, QK_PAD, tq), lambda b, h, i: (b, h, i, 0, 0)),
            pl.BlockSpec((1, hb, cfg.seq, QK_PAD), lambda b, h, i: (b, h, 0, 0)),
            pl.BlockSpec((1, hb, nq, V_HEAD, tq), lambda b, h, i: (b, h, 0, 0, 0)),
        ],
        out_specs=pl.BlockSpec((1, tq, hb * V_HEAD), lambda b, h, i: (b, i, h)),
        out_shape=jax.ShapeDtypeStruct((cfg.batch, cfg.seq, hds * V_HEAD), BF16),
        scratch_shapes=[pltpu.VMEM((hb, 1, tq), F32), pltpu.VMEM((hb, 1, tq), F32),
                        pltpu.VMEM((hb, V_HEAD, tq), F32),
                        pltpu.VMEM((hb, tq // 2, tq), F32), pltpu.VMEM((hb, tq // 2, tq), F32),
                        pltpu.VMEM((hb, 1, tq), F32), pltpu.VMEM((hb, 1, tq), F32)],
        compiler_params=_cparams(("parallel", "parallel", "arbitrary"), t.vmem_mb),
        name="mla_flash_attn",
    )(q, k, vt)


def _pad_lanes(v, width=LANES):
    v = v.reshape(1, -1)
    return jnp.pad(v, ((0, 0), (0, width - v.shape[1])))


def _forward(cfg, t, x, p, positions, ln_mix, ln_mlp, mlp_w1, mlp_w2, ple_norm, ple_gate_w, ple_proj_w,
             ssm_in_w, ssm_conv_w, ssm_conv_b, ssm_dt_bias, ssm_a_log, ssm_d, ssm_norm, ssm_out_w,
             kv_in_norm, w_dkv, ckv_norm, w_kr, w_uk, w_uv, k_nope_norm, k_rope_norm,
             w_dq, cq_norm, w_uq, q_nope_norm, q_rope_norm, mla_out_w):
    depth = ln_mix.shape[0]
    n_a = ssm_in_w.shape[0]
    m = cfg.batch * cfg.seq
    d = cfg.d_model
    hds = cfg.mla_heads
    row = lambda v: v.reshape(1, -1).astype(F32)
    bf = lambda w: w.astype(BF16)
    h = x.reshape(m, d)
    p2 = p.reshape(depth, m, cfg.ple_dim)

    kv_ready = False
    for i in range(depth):
        if i < n_a:
            j = i
            d_inner = cfg.d_inner
            cdim = d_inner + 2 * cfg.ssm_groups * SSM_STATE
            w_dt = lax.slice(ssm_in_w, (j, 0, d_inner + cdim), (j + 1, d, d_inner + cdim + d_inner // HEADDIM))
            w_dt = bf(jnp.pad(w_dt[0], ((0, 0), (0, LANES - d_inner // HEADDIM))))
            zx, dt_raw = _inproj(h, row(ln_mix[i]), ssm_in_w, j, d_inner + cdim, w_dt, t)
            cwb = jnp.concatenate([ssm_conv_w[j], ssm_conv_b[j][None, :],
                                   jnp.zeros((SUBLANES - CONV_W - 1, cdim), F32)], axis=0)
            cwb = cwb.reshape(SUBLANES, cdim // LANES, LANES).transpose(1, 0, 2)
            yn = _ssd(zx, dt_raw, cwb, _pad_lanes(ssm_dt_bias[j]), _pad_lanes(ssm_a_log[j]),
                      row(jnp.repeat(ssm_d[j], HEADDIM)), row(ssm_norm[j]), cfg, t)
            h = _matmul_res(yn, ssm_out_w, j, h, t, "ssm_out_proj")
        else:
            j = i - n_a
            if not kv_ready:
                kv_h = h
                kv_ready = True
            half = QK_ROPE // 2
            fr = ROPE_THETA ** (-jnp.arange(half, dtype=F32) / half)
            freqs = _pad_lanes(jnp.concatenate([fr, fr]))
            sign = _pad_lanes(jnp.concatenate([-jnp.ones((half,), F32), jnp.ones((half,), F32)]))
            w_kr_p = bf(jnp.pad(w_kr, ((0, 0), (0, LANES - QK_ROPE))))
            assert depth - n_a == 1
            ckv, cq, kr, cos_t, sin_t = _mla_a(
                kv_h, row(kv_in_norm), row(ln_mix[i]), bf(w_dkv), w_kr_p, bf(w_dq[j]), row(ckv_norm),
                _pad_lanes(k_rope_norm), row(cq_norm[j]), positions.reshape(m, 1), freqs, sign, t)
            wq = w_uq[j].reshape(cfg.q_lora, hds, QK_NOPE + QK_ROPE)
            wq = jnp.pad(wq, ((0, 0), (0, 0), (0, QK_PAD - QK_NOPE - QK_ROPE))).reshape(cfg.q_lora, hds * QK_PAD)
            q, k, vt = _mla_b(ckv, cq, kr, cos_t, sin_t, bf(w_uk), bf(w_uv.T), bf(wq.T), row(k_nope_norm),
                              q_nope_norm[j].reshape(-1, 1), q_rope_norm[j].reshape(-1, 1), cfg, t)
            o = _attn(q, k, vt, cfg, t)
            h = _matmul_res(o.reshape(m, hds * V_HEAD), bf(mla_out_w), j, h, t, "mla_out_proj")
        h = _mlp(h, row(ln_mlp[i]), mlp_w1, mlp_w2, i, t)
        h = _ple(h, row(ple_norm[i]), bf(ple_gate_w[i]), p2, i, bf(ple_proj_w[i]), t)
    return h.reshape(cfg.batch, cfg.seq, d)


_CFG = _Cfg(batch=2, seq=4096, d_model=2048, d_ff=8192, ple_dim=256, d_inner=4096, ssm_groups=8,
            mla_heads=16, q_lora=512, kv_lora=512)
_TILES = _Tiles(tm_proj=1024, tn_proj=512, tn_in=1024, tm_mlp=1024, tf_mlp=512, tm_ple=512, tm_mla=512, t_attn=512,
                attn_heads=4, vmem_mb=56)


def kernel(x, p, positions, ln_mix, ln_mlp, mlp_w1, mlp_w2, ple_norm, ple_gate_w, ple_proj_w, ssm_in_w, ssm_conv_w, ssm_conv_b, ssm_dt_bias, ssm_a_log, ssm_d, ssm_norm, ssm_out_w, kv_in_norm, w_dkv, ckv_norm, w_kr, w_uk, w_uv, k_nope_norm, k_rope_norm, w_dq, cq_norm, w_uq, q_nope_norm, q_rope_norm, mla_out_w):
    return _forward(_CFG, _TILES, x, p, positions, ln_mix, ln_mlp, mlp_w1, mlp_w2, ple_norm, ple_gate_w,
                    ple_proj_w, ssm_in_w, ssm_conv_w, ssm_conv_b, ssm_dt_bias, ssm_a_log, ssm_d, ssm_norm,
                    ssm_out_w, kv_in_norm, w_dkv, ckv_norm, w_kr, w_uk, w_uv, k_nope_norm, k_rope_norm,
                    w_dq, cq_norm, w_uq, q_nope_norm, q_rope_norm, mla_out_w)
```

```python
import functools
import math
from typing import NamedTuple

import jax
import jax.numpy as jnp
from jax import lax
from jax.experimental import pallas as pl
from jax.experimental.pallas import tpu as pltpu

F32 = jnp.float32
BF16 = jnp.bfloat16
EPS = 1e-6

LANES = 128
SUBLANES = 8
SSD_CHUNK = 128
HEADDIM = 64
SSM_STATE = 128
CONV_W = 4
CONV_HIST = 8
CONV_STRIDE = 4
MASK_CHUNK_SHIFT = 6
QK_NOPE = 128
QK_ROPE = 64
V_HEAD = 128
QK_PAD = 256
ROPE_THETA = 10000.0
LOG2E = math.log2(math.e)
MIB = 1024 * 1024


class _Cfg(NamedTuple):
    batch: int
    seq: int
    d_model: int
    d_ff: int
    ple_dim: int
    d_inner: int
    ssm_groups: int
    mla_heads: int
    q_lora: int
    kv_lora: int


class _Tiles(NamedTuple):
    tm_proj: int
    tn_proj: int
    tn_in: int
    tm_mlp: int
    tf_mlp: int
    tm_ple: int
    tm_mla: int
    t_attn: int
    attn_heads: int
    vmem_mb: int


def _cparams(sem, vmem_mb):
    return pltpu.CompilerParams(dimension_semantics=sem, vmem_limit_bytes=vmem_mb * MIB)


def _rms(x, g):
    return x * lax.rsqrt(jnp.mean(x * x, axis=-1, keepdims=True) + EPS) * g


def _sigmoid(x):
    return 1.0 / (1.0 + jnp.exp2(x * (-LOG2E)))


def _softplus(x):
    return jnp.maximum(x, 0.0) + jnp.log1p(jnp.exp(-jnp.abs(x)))


def _dot(a, b):
    return jnp.dot(a, b, preferred_element_type=F32)


def _dot_nt(a, b):
    return lax.dot_general(a, b, (((1,), (1,)), ((), ())), preferred_element_type=F32)


def _split3(v):
    hi = v.astype(BF16)
    r = v - hi.astype(F32)
    mid = r.astype(BF16)
    lo = (r - mid.astype(F32)).astype(BF16)
    return hi, mid, lo


def _inproj_kernel(h_ref, g_ref, w_ref, wdt_ref, zx_ref, dt_ref, hn_ref):
    @pl.when(pl.program_id(1) == 0)
    def _():
        hn = _rms(h_ref[...], g_ref[...]).astype(BF16)
        hn_ref[...] = hn
        dt_ref[...] = _dot_nt(hn, wdt_ref[...])

    zx_ref[...] = _dot_nt(hn_ref[...], w_ref[...].astype(BF16)).astype(zx_ref.dtype)


def _inproj(h, gain, w_in_t, layer, n, w_dt, t):
    m, k = h.shape
    tm, tn = t.tm_proj, t.tn_in
    return pl.pallas_call(
        _inproj_kernel,
        grid=(m // tm, n // tn),
        in_specs=[
            pl.BlockSpec((tm, k), lambda i, j: (i, 0), pipeline_mode=pl.Buffered(1)),
            pl.BlockSpec((1, k), lambda i, j: (0, 0)),
            pl.BlockSpec((None, tn, k), lambda i, j: (layer, j, 0)),
            pl.BlockSpec((LANES, k), lambda i, j: (0, 0)),
        ],
        out_specs=[
            pl.BlockSpec((tm, tn), lambda i, j: (i, j)),
            pl.BlockSpec((tm, LANES), lambda i, j: (i, 0)),
        ],
        out_shape=[jax.ShapeDtypeStruct((m, n), BF16), jax.ShapeDtypeStruct((m, LANES), F32)],
        scratch_shapes=[pltpu.VMEM((tm, k), BF16)],
        compiler_params=_cparams(("parallel", "arbitrary"), t.vmem_mb),
        name="ssm_in_proj",
    )(h, gain, w_in_t, w_dt)


def _ssd_kernel(z_ref, x_ref, b_ref, c_ref, dt_ref, cwb_ref, dtb_ref, alog_ref, dfull_ref, ng_ref, e_ref,
                out_ref, buf_ref, xc_ref, state_ref, *, d_inner, gn, groups):
    ln = SSD_CHUNK
    cdim = d_inner + 2 * gn
    nslab = cdim // LANES
    sl_b = d_inner // LANES
    sl_c = (d_inner + gn) // LANES
    heads = d_inner // HEADDIM
    hg = heads // groups
    gw = hg * HEADDIM
    c = pl.program_id(1)

    @pl.when(c == 0)
    def _():
        buf_ref[:, 0:CONV_HIST, :] = jnp.zeros((nslab, CONV_HIST, LANES), F32)
        state_ref[...] = jnp.zeros_like(state_ref)

    @pl.when(c > 0)
    def _():
        buf_ref[:, 0:CONV_HIST, :] = buf_ref[:, ln:ln + CONV_HIST, :]

    for s in range(nslab):
        lo = s * LANES
        if lo < d_inner:
            src = x_ref[:, lo:lo + LANES]
        elif lo < d_inner + gn:
            src = b_ref[:, lo - d_inner:lo - d_inner + LANES]
        else:
            src = c_ref[:, lo - d_inner - gn:lo - d_inner - gn + LANES]
        buf_ref[s, CONV_HIST:CONV_HIST + ln, :] = src.astype(F32)

    rows_per_blk = SUBLANES * CONV_STRIDE

    def conv_slab(s, carry):
        wb = cwb_ref[s]
        for blk in range(ln // rows_per_blk):
            for i in range(CONV_STRIDE):
                t0 = blk * rows_per_blk + i
                acc = wb[CONV_W:CONV_W + 1, :]
                for k in range(CONV_W):
                    start = CONV_HIST + t0 - (CONV_W - 1) + k
                    acc = acc + wb[k:k + 1, :] * buf_ref[s, pl.ds(start, SUBLANES, stride=CONV_STRIDE), :]
                xc_ref[s, pl.ds(t0, SUBLANES, stride=CONV_STRIDE), :] = acc * _sigmoid(acc)
        return carry

    lax.fori_loop(0, nslab, conv_slab, 0, unroll=8)

    dt = _softplus(dt_ref[...] + dtb_ref[...])
    a = dt * (-jnp.exp(alog_ref[...])) * LOG2E
    row = lax.broadcasted_iota(jnp.int32, (ln, ln), 0)
    col = lax.broadcasted_iota(jnp.int32, (ln, ln), 1)
    causal = row >= col
    tril = jnp.where(causal, 1.0, 0.0).astype(BF16)
    a_hi, a_mid, a_lo = _split3(a)
    acs = _dot(tril, a_hi) + _dot(tril, a_mid) + _dot(tril, a_lo)
    acs_t = acs.T
    last = acs[ln - 1:ln, :]
    eacs = jnp.exp2(acs)
    wend = jnp.exp2(last - acs) * dt
    lo_half = lax.broadcasted_iota(jnp.int32, (ln, LANES), 1) < HEADDIM

    def hi_mid(v):
        hi = v.astype(BF16)
        return jnp.concatenate([hi, (v - hi.astype(F32)).astype(BF16)], axis=1)

    per_head = jnp.concatenate([hi_mid(eacs), hi_mid(wend), hi_mid(dt)], axis=0)

    for g in range(groups):
        gcols = slice(g * gw, (g + 1) * gw)
        bg = xc_ref[sl_b + g]
        cg = xc_ref[sl_c + g].astype(BF16)
        cb = _dot_nt(cg, bg.astype(BF16))
        bt = bg.T.astype(BF16)
        eg = e_ref[:, gcols]
        xg = jnp.concatenate([xc_ref[g * (gw // LANES) + q] for q in range(gw // LANES)], axis=1)
        st = state_ref[:, gcols]
        lanes_g = _dot(per_head, eg)
        eacs_g = lanes_g[0:ln]
        y_off = _dot(cg, st.astype(BF16)) * eacs_g
        s_new = _dot(bt, (xg * lanes_g[ln:2 * ln]).astype(BF16))
        state_ref[:, gcols] = st * eacs_g[ln - 1:ln, :] + s_new
        xdt = xg * lanes_g[2 * ln:3 * ln]
        ys = []
        for p in range(hg // 2):
            pr = (g * hg) // 2 + p
            m_parts = []
            for hh in (2 * pr, 2 * pr + 1):
                a_row = acs_t[hh:hh + 1, :]
                a_col = jnp.broadcast_to(acs[:, hh:hh + 1], (ln, ln))
                dec = jnp.exp2(jnp.where(causal, a_col - a_row, -jnp.inf))
                m_parts.append((cb * dec).astype(BF16))
            x2 = xdt[:, p * LANES:(p + 1) * LANES]
            xa = jnp.where(lo_half, x2, 0.0).astype(BF16)
            xb = jnp.where(lo_half, 0.0, x2).astype(BF16)
            ys.append(_dot(jnp.concatenate(m_parts, axis=1), jnp.concatenate([xa, xb], axis=0)))
        yg = jnp.concatenate(ys, axis=1) + y_off + dfull_ref[:, gcols] * xg
        zg = z_ref[:, gcols].astype(F32)
        yg = yg * (zg * _sigmoid(zg))
        out_ref[:, gcols] = _rms(yg, ng_ref[:, gcols]).astype(out_ref.dtype)


def _ssd(zx, dt_raw, conv_wb, dt_bias, a_log, d_full, norm_g, cfg, t):
    ln = SSD_CHUNK
    d_inner = cfg.d_inner
    gn = cfg.ssm_groups * SSM_STATE
    cdim = d_inner + 2 * gn
    nslab = cdim // LANES
    nc = cfg.seq // ln
    m = cfg.batch * cfg.seq
    b_blk = (2 * d_inner) // gn
    row = lambda b, c: b * nc + c
    kern = functools.partial(_ssd_kernel, d_inner=d_inner, gn=gn, groups=cfg.ssm_groups)
    head_lanes = (jnp.arange(d_inner)[None, :] // HEADDIM == jnp.arange(LANES)[:, None]).astype(BF16)
    head_lanes = jnp.concatenate([head_lanes, head_lanes], axis=0)
    const = lambda b, c: (0, 0)
    return pl.pallas_call(
        kern,
        grid=(cfg.batch, nc),
        in_specs=[
            pl.BlockSpec((ln, d_inner), lambda b, c: (row(b, c), 0)),
            pl.BlockSpec((ln, d_inner), lambda b, c: (row(b, c), 1)),
            pl.BlockSpec((ln, gn), lambda b, c: (row(b, c), b_blk)),
            pl.BlockSpec((ln, gn), lambda b, c: (row(b, c), b_blk + 1)),
            pl.BlockSpec((ln, LANES), lambda b, c: (row(b, c), 0)),
            pl.BlockSpec((nslab, SUBLANES, LANES), lambda b, c: (0, 0, 0)),
            pl.BlockSpec((1, LANES), const),
            pl.BlockSpec((1, LANES), const),
            pl.BlockSpec((1, d_inner), const),
            pl.BlockSpec((1, d_inner), const),
            pl.BlockSpec((2 * LANES, d_inner), const),
        ],
        out_specs=pl.BlockSpec((ln, d_inner), lambda b, c: (row(b, c), 0)),
        out_shape=jax.ShapeDtypeStruct((m, d_inner), BF16),
        scratch_shapes=[
            pltpu.VMEM((nslab, CONV_HIST + ln, LANES), F32),
            pltpu.VMEM((nslab, ln, LANES), F32),
            pltpu.VMEM((SSM_STATE, d_inner), F32),
        ],
        compiler_params=_cparams(("parallel", "arbitrary"), t.vmem_mb),
        name="ssm_scan",
    )(zx, zx, zx, zx, dt_raw, conv_wb, dt_bias, a_log, d_full, norm_g, head_lanes)


def _matmul_res_kernel(a_ref, w_ref, r_ref, o_ref):
    o_ref[...] = r_ref[...] + _dot(a_ref[...], w_ref[...].astype(BF16))


def _matmul_res(a, w, layer, res, t, name):
    m, k = a.shape
    n = w.shape[2]
    tm, tn = t.tm_proj, min(t.tn_proj, n)
    return pl.pallas_call(
        _matmul_res_kernel,
        grid=(m // tm, n // tn),
        in_specs=[
            pl.BlockSpec((tm, k), lambda i, j: (i, 0)),
            pl.BlockSpec((None, k, tn), lambda i, j: (layer, 0, j)),
            pl.BlockSpec((tm, tn), lambda i, j: (i, j)),
        ],
        out_specs=pl.BlockSpec((tm, tn), lambda i, j: (i, j)),
        out_shape=jax.ShapeDtypeStruct((m, n), F32),
        compiler_params=_cparams(("parallel", "parallel"), t.vmem_mb),
        name=name,
    )(a, w, res)


def _mlp_kernel(h_ref, g_ref, w1_ref, w2_ref, o_ref, hn_ref):
    @pl.when(pl.program_id(1) == 0)
    def _():
        h = h_ref[...]
        hn_ref[...] = _rms(h, g_ref[...]).astype(BF16)
        o_ref[...] = h

    a = jnp.maximum(_dot(hn_ref[...], w1_ref[...].astype(BF16)), 0.0)
    o_ref[...] += _dot((a * a).astype(BF16), w2_ref[...].astype(BF16))


def _mlp(h, gain, w1, w2, layer, t):
    m, d = h.shape
    f = w1.shape[2]
    tm, tf = t.tm_mlp, t.tf_mlp
    return pl.pallas_call(
        _mlp_kernel,
        grid=(m // tm, f // tf),
        in_specs=[
            pl.BlockSpec((tm, d), lambda i, j: (i, 0), pipeline_mode=pl.Buffered(1)),
            pl.BlockSpec((1, d), lambda i, j: (0, 0)),
            pl.BlockSpec((None, d, tf), lambda i, j: (layer, 0, j)),
            pl.BlockSpec((None, tf, d), lambda i, j: (layer, j, 0)),
        ],
        out_specs=pl.BlockSpec((tm, d), lambda i, j: (i, 0)),
        out_shape=jax.ShapeDtypeStruct((m, d), F32),
        scratch_shapes=[pltpu.VMEM((tm, d), BF16)],
        compiler_params=_cparams(("parallel", "arbitrary"), t.vmem_mb),
        name="sq_relu_mlp",
    )(h, gain, w1, w2)


def _ple_kernel(h_ref, g_ref, wg_ref, p_ref, wp_ref, o_ref, *, tn):
    hn = _rms(h_ref[...], g_ref[...]).astype(BF16)
    pb = p_ref[...].astype(BF16)
    for j in range(h_ref.shape[1] // tn):
        cols = slice(j * tn, (j + 1) * tn)
        gate = _sigmoid(_dot(hn, wg_ref[:, cols]))
        o_ref[:, cols] = h_ref[:, cols] + gate * _dot(pb, wp_ref[:, cols])


def _ple(h, gain, wg, p, layer, wp, t):
    m, d = h.shape
    pd = p.shape[2]
    tm = t.tm_ple
    tn = 512 if d % 512 == 0 else LANES
    return pl.pallas_call(
        functools.partial(_ple_kernel, tn=tn),
        grid=(m // tm,),
        in_specs=[
            pl.BlockSpec((tm, d), lambda i: (i, 0)),
            pl.BlockSpec((1, d), lambda i: (0, 0)),
            pl.BlockSpec((d, d), lambda i: (0, 0)),
            pl.BlockSpec((None, tm, pd), lambda i: (layer, i, 0)),
            pl.BlockSpec((pd, d), lambda i: (0, 0)),
        ],
        out_specs=pl.BlockSpec((tm, d), lambda i: (i, 0)),
        out_shape=jax.ShapeDtypeStruct((m, d), F32),
        compiler_params=_cparams(("parallel",), t.vmem_mb),
        name="per_layer_input",
    )(h, gain, wg, p, wp)


def _rope(x, cos_f, sin_s):
    lane = lax.broadcasted_iota(jnp.int32, x.shape, 1)
    half = QK_ROPE // 2
    rot = jnp.where(lane < half, pltpu.roll(x, LANES - half, 1), pltpu.roll(x, half, 1))
    return x * cos_f + rot * sin_s


def _rms_rope_part(x, g_pad):
    ms = jnp.sum(x * x, axis=-1, keepdims=True) * (1.0 / QK_ROPE)
    return x * lax.rsqrt(ms + EPS) * g_pad


def _mla_a_kernel(h_ref, gkv_ref, gq_ref, wdkv_ref, wkr_ref, wdq_ref, ckvn_ref, krn_ref, cqn_ref,
                  pos_ref, fr_ref, ckv_ref, cq_ref, kr_ref, cos_ref, sin_ref):
    h = h_ref[...]
    y = h * lax.rsqrt(jnp.mean(h * h, axis=-1, keepdims=True) + EPS)
    hkv = (y * gkv_ref[...]).astype(BF16)
    hq = (y * gq_ref[...]).astype(BF16)
    ckv_ref[...] = _rms(_dot(hkv, wdkv_ref[...]), ckvn_ref[...]).astype(BF16)
    cq_ref[...] = _rms(_dot(hq, wdq_ref[...]), cqn_ref[...]).astype(BF16)
    ang = fr_ref[...] * pos_ref[...].astype(F32)
    cos_t = jnp.cos(ang)
    sin_t = jnp.sin(ang)
    cos_ref[...] = cos_t
    sin_ref[...] = sin_t
    pad = jnp.zeros((LANES - QK_ROPE, ang.shape[1]), F32)
    cos_f = jnp.concatenate([cos_t, cos_t, pad], axis=0).T
    sin_s = jnp.concatenate([-sin_t, sin_t, pad], axis=0).T
    kr = _rms_rope_part(_dot(hkv, wkr_ref[...]), krn_ref[...])
    kr_ref[...] = _rope(kr, cos_f, sin_s).astype(BF16)


def _mla_a(h, g_kv, g_q, w_dkv, w_kr, w_dq, ckv_n, kr_n, cq_n, pos, freqs, t):
    m, d = h.shape
    kvl = w_dkv.shape[1]
    ql = w_dq.shape[1]
    tm = t.tm_mla
    rowblk = lambda w: pl.BlockSpec((tm, w), lambda i: (i, 0))
    full = lambda a: pl.BlockSpec(a.shape, lambda i: (0, 0))
    colblk = pl.BlockSpec((QK_ROPE // 2, tm), lambda i: (0, i))
    return pl.pallas_call(
        _mla_a_kernel,
        grid=(m // tm,),
        in_specs=[rowblk(d), full(g_kv), full(g_q), full(w_dkv), full(w_kr), full(w_dq), full(ckv_n),
                  full(kr_n), full(cq_n), pl.BlockSpec((1, tm), lambda i: (0, i)), full(freqs)],
        out_specs=[rowblk(kvl), rowblk(ql), rowblk(LANES), colblk, colblk],
        out_shape=[jax.ShapeDtypeStruct((m, kvl), BF16), jax.ShapeDtypeStruct((m, ql), BF16),
                   jax.ShapeDtypeStruct((m, LANES), BF16), jax.ShapeDtypeStruct((QK_ROPE // 2, m), F32),
                   jax.ShapeDtypeStruct((QK_ROPE // 2, m), F32)],
        compiler_params=_cparams(("parallel",), t.vmem_mb),
        name="mla_down_proj",
    )(h, g_kv, g_q, w_dkv, w_kr, w_dq, ckv_n, kr_n, cq_n, pos, freqs)


def _mla_b_kernel(ckv_ref, cq_ref, kr_ref, cos_ref, sin_ref, wuk_ref, wuvt_ref, wuqt_ref, knn_ref, qnn_ref,
                  qrn_ref, q_ref, k_ref, vt_ref, *, scale, heads):
    ckv = ckv_ref[...]
    cq = cq_ref[...]
    kr = kr_ref[...]
    cos_t = cos_ref[...]
    sin_t = sin_ref[...]
    half = QK_ROPE // 2
    g_nope = qnn_ref[...] * scale
    g_rope = qrn_ref[...] * scale
    for h in range(heads):
        kn = _rms(_dot(ckv, wuk_ref[:, h * QK_NOPE:(h + 1) * QK_NOPE]), knn_ref[...])
        k_ref[0, h, :, 0:QK_NOPE] = kn.astype(BF16)
        k_ref[0, h, :, QK_NOPE:QK_PAD] = kr
        vt_ref[0, h, 0] = _dot_nt(wuvt_ref[h * V_HEAD:(h + 1) * V_HEAD, :], ckv).astype(BF16)
        qt = _dot_nt(wuqt_ref[h * QK_PAD:(h + 1) * QK_PAD, :], cq)
        qn = qt[0:QK_NOPE]
        qn = qn * lax.rsqrt(jnp.mean(qn * qn, axis=0, keepdims=True) + EPS) * g_nope
        qr = qt[QK_NOPE:QK_NOPE + QK_ROPE]
        qr = qr * lax.rsqrt(jnp.mean(qr * qr, axis=0, keepdims=True) + EPS) * g_rope
        x1 = qr[0:half]
        x2 = qr[half:QK_ROPE]
        q_ref[0, h, 0, 0:QK_NOPE, :] = qn.astype(BF16)
        q_ref[0, h, 0, QK_NOPE:QK_NOPE + half, :] = (x1 * cos_t - x2 * sin_t).astype(BF16)
        q_ref[0, h, 0, QK_NOPE + half:QK_NOPE + QK_ROPE, :] = (x2 * cos_t + x1 * sin_t).astype(BF16)
        q_ref[0, h, 0, QK_NOPE + QK_ROPE:QK_PAD, :] = jnp.zeros((QK_PAD - QK_NOPE - QK_ROPE, qt.shape[1]), BF16)


def _mla_b(ckv, cq, kr, cos_t, sin_t, w_uk, w_uvt, w_uqt, kn_n, qn_n, qr_n, cfg, t):
    m, kvl = ckv.shape
    ql = cq.shape[1]
    hds = cfg.mla_heads
    tm = t.t_attn
    ns = cfg.seq // tm
    scale = (QK_NOPE + QK_ROPE) ** -0.5 * LOG2E
    rowblk = lambda w: pl.BlockSpec((tm, w), lambda i: (i, 0))
    full = lambda a: pl.BlockSpec(a.shape, lambda i: (0, 0))
    colblk = pl.BlockSpec((QK_ROPE // 2, tm), lambda i: (0, i))
    return pl.pallas_call(
        functools.partial(_mla_b_kernel, scale=scale, heads=hds),
        grid=(m // tm,),
        in_specs=[rowblk(kvl), rowblk(ql), rowblk(LANES), colblk, colblk,
                  full(w_uk), full(w_uvt), full(w_uqt), full(kn_n), full(qn_n), full(qr_n)],
        out_specs=[
            pl.BlockSpec((1, hds, 1, QK_PAD, tm), lambda i: (i // ns, 0, i % ns, 0, 0)),
            pl.BlockSpec((1, hds, tm, QK_PAD), lambda i: (i // ns, 0, i % ns, 0)),
            pl.BlockSpec((1, hds, 1, V_HEAD, tm), lambda i: (i // ns, 0, i % ns, 0, 0)),
        ],
        out_shape=[jax.ShapeDtypeStruct((cfg.batch, hds, ns, QK_PAD, tm), BF16),
                   jax.ShapeDtypeStruct((cfg.batch, hds, cfg.seq, QK_PAD), BF16),
                   jax.ShapeDtypeStruct((cfg.batch, hds, ns, V_HEAD, tm), BF16)],
        compiler_params=_cparams(("parallel",), t.vmem_mb),
        name="mla_up_proj",
    )(ckv, cq, kr, cos_t, sin_t, w_uk, w_uvt, w_uqt, kn_n, qn_n, qr_n)


def _attn_kernel(q_ref, k_ref, vt_ref, o_ref, m_ref, l_ref, acc_ref, sa_ref, sb_ref, bma_ref, bmb_ref, *, tq, hb):
    qi = pl.program_id(2)
    tk = tq // 2
    m_ref[...] = jnp.full_like(m_ref, -jnp.inf)
    l_ref[...] = jnp.zeros_like(l_ref)
    acc_ref[...] = jnp.zeros_like(acc_ref)

    def visible_mask(k0):
        krow = k0 + lax.broadcasted_iota(jnp.int32, (tk, tq), 0)
        qcol = qi * tq + lax.broadcasted_iota(jnp.int32, (tk, tq), 1)
        return (krow >> MASK_CHUNK_SHIFT) <= (qcol >> MASK_CHUNK_SHIFT)

    def qk_stage(j, half, s_ref, bm_ref, masked):
        k0 = j * tq + half * tk
        if masked:
            visible = visible_mask(k0)
        for hh in range(hb):
            k = k_ref[0, hh, pl.ds(pl.multiple_of(k0, tk), tk), :]
            s = _dot(k, q_ref[0, hh, 0])
            if masked:
                s = jnp.where(visible, s, -jnp.inf)
            s_ref[hh] = s
            bm_ref[hh] = jnp.max(s, axis=0, keepdims=True)

    def sm_stage(j, half, s_ref, bm_ref, masked=False):
        if masked:
            visible = visible_mask(j * tq + half * tk)
        for hh in range(hb):
            m_prev = m_ref[hh]
            if masked:
                s = jnp.where(visible, s_ref[hh], -jnp.inf)
                m_new = jnp.maximum(m_prev, jnp.max(s, axis=0, keepdims=True))
            else:
                s = s_ref[hh]
                m_new = jnp.maximum(m_prev, bm_ref[hh])
            p = jnp.exp2(s - m_new)
            alpha = jnp.exp2(m_prev - m_new)
            l_ref[hh] = alpha * l_ref[hh] + jnp.sum(p, axis=0, keepdims=True)
            vt = vt_ref[0, hh, j, :, half * tk:(half + 1) * tk]
            acc_ref[hh] = acc_ref[hh] * alpha + _dot(vt, p.astype(BF16))
            m_ref[hh] = m_new

    qk_stage(0, 0, sa_ref, bma_ref, False)

    def body(j, carry):
        qk_stage(j, 1, sb_ref, bmb_ref, False)
        sm_stage(j, 0, sa_ref, bma_ref)
        qk_stage(j + 1, 0, sa_ref, bma_ref, False)
        sm_stage(j, 1, sb_ref, bmb_ref)
        return carry

    lax.fori_loop(0, qi, body, 0)
    k1 = qi * tq + tk
    hi_q = slice(tk, tq)
    krow = k1 + lax.broadcasted_iota(jnp.int32, (tk, tk), 0)
    qcol = k1 + lax.broadcasted_iota(jnp.int32, (tk, tk), 1)
    visible_q = (krow >> MASK_CHUNK_SHIFT) <= (qcol >> MASK_CHUNK_SHIFT)
    for hh in range(hb):
        k = k_ref[0, hh, pl.ds(pl.multiple_of(k1, tk), tk), :]
        s = _dot(k, q_ref[0, hh, 0, :, hi_q])
        sb_ref[hh, :, hi_q] = jnp.where(visible_q, s, -jnp.inf)
    sm_stage(qi, 0, sa_ref, bma_ref, masked=True)
    for hh in range(hb):
        s = sb_ref[hh, :, hi_q]
        m_prev = m_ref[hh, :, hi_q]
        m_new = jnp.maximum(m_prev, jnp.max(s, axis=0, keepdims=True))
        p = jnp.exp2(s - m_new)
        alpha = jnp.exp2(m_prev - m_new)
        l_ref[hh, :, hi_q] = alpha * l_ref[hh, :, hi_q] + jnp.sum(p, axis=0, keepdims=True)
        vt = vt_ref[0, hh, qi, :, hi_q]
        acc_ref[hh, :, hi_q] = acc_ref[hh, :, hi_q] * alpha + _dot(vt, p.astype(BF16))
        m_ref[hh, :, hi_q] = m_new
    for hh in range(hb):
        o = acc_ref[hh] / l_ref[hh]
        o_ref[0, :, hh * V_HEAD:(hh + 1) * V_HEAD] = o.T.astype(o_ref.dtype)


def _attn(q, k, vt, cfg, t):
    tq = t.t_attn
    hb = t.attn_heads
    hds = cfg.mla_heads
    nq = cfg.seq // tq
    return pl.pallas_call(
        functools.partial(_attn_kernel, tq=tq, hb=hb),
        grid=(cfg.batch, hds // hb, nq),
        in_specs=[
            pl.BlockSpec((1, hb, 1, QK_PAD, tq), lambda b, h, i: (b, h, i, 0, 0)),
            pl.BlockSpec((1, hb, cfg.seq, QK_PAD), lambda b, h, i: (b, h, 0, 0)),
            pl.BlockSpec((1, hb, nq, V_HEAD, tq), lambda b, h, i: (b, h, 0, 0, 0)),
        ],
        out_specs=pl.BlockSpec((1, tq, hb * V_HEAD), lambda b, h, i: (b, i, h)),
        out_shape=jax.ShapeDtypeStruct((cfg.batch, cfg.seq, hds * V_HEAD), BF16),
        scratch_shapes=[pltpu.VMEM((hb, 1, tq), F32), pltpu.VMEM((hb, 1, tq), F32),
                        pltpu.VMEM((hb, V_HEAD, tq), F32),
                        pltpu.VMEM((hb, tq // 2, tq), F32), pltpu.VMEM((hb, tq // 2, tq), F32),
                        pltpu.VMEM((hb, 1, tq), F32), pltpu.VMEM((hb, 1, tq), F32)],
        compiler_params=_cparams(("parallel", "parallel", "arbitrary"), t.vmem_mb),
        name="mla_flash_attn",
    )(q, k, vt)


def _pad_lanes(v, width=LANES):
    v = v.reshape(1, -1)
    return jnp.pad(v, ((0, 0), (0, width - v.shape[1])))


def _forward(cfg, t, x, p, positions, ln_mix, ln_mlp, mlp_w1, mlp_w2, ple_norm, ple_gate_w, ple_proj_w,
             ssm_in_w, ssm_conv_w, ssm_conv_b, ssm_dt_bias, ssm_a_log, ssm_d, ssm_norm, ssm_out_w,
             kv_in_norm, w_dkv, ckv_norm, w_kr, w_uk, w_uv, k_nope_norm, k_rope_norm,
             w_dq, cq_norm, w_uq, q_nope_norm, q_rope_norm, mla_out_w):
    depth = ln_mix.shape[0]
    n_a = ssm_in_w.shape[0]
    m = cfg.batch * cfg.seq
    d = cfg.d_model
    hds = cfg.mla_heads
    row = lambda v: v.reshape(1, -1).astype(F32)
    bf = lambda w: w.astype(BF16)
    h = x.reshape(m, d)
    p2 = p.reshape(depth, m, cfg.ple_dim)

    kv_ready = False
    for i in range(depth):
        if i < n_a:
            j = i
            d_inner = cfg.d_inner
            cdim = d_inner + 2 * cfg.ssm_groups * SSM_STATE
            w_in_t = jnp.swapaxes(ssm_in_w, 1, 2)
            w_dt = bf(jnp.pad(w_in_t[j, d_inner + cdim:], ((0, LANES - d_inner // HEADDIM), (0, 0))))
            zx, dt_raw = _inproj(h, row(ln_mix[i]), w_in_t, j, d_inner + cdim, w_dt, t)
            cwb = jnp.concatenate([ssm_conv_w[j], ssm_conv_b[j][None, :],
                                   jnp.zeros((SUBLANES - CONV_W - 1, cdim), F32)], axis=0)
            cwb = cwb.reshape(SUBLANES, cdim // LANES, LANES).transpose(1, 0, 2)
            yn = _ssd(zx, dt_raw, cwb, _pad_lanes(ssm_dt_bias[j]), _pad_lanes(ssm_a_log[j]),
                      row(jnp.repeat(ssm_d[j], HEADDIM)), row(ssm_norm[j]), cfg, t)
            h = _matmul_res(yn, ssm_out_w, j, h, t, "ssm_out_proj")
        else:
            j = i - n_a
            if not kv_ready:
                kv_h = h
                kv_ready = True
            half = QK_ROPE // 2
            fr = ROPE_THETA ** (-jnp.arange(half, dtype=F32) / half)
            freqs = fr.reshape(half, 1)
            w_kr_p = bf(jnp.pad(w_kr, ((0, 0), (0, LANES - QK_ROPE))))
            assert depth - n_a == 1
            ckv, cq, kr, cos_t, sin_t = _mla_a(
                kv_h, row(kv_in_norm), row(ln_mix[i]), bf(w_dkv), w_kr_p, bf(w_dq[j]), row(ckv_norm),
                _pad_lanes(k_rope_norm), row(cq_norm[j]), positions.reshape(1, m), freqs, t)
            wq = w_uq[j].reshape(cfg.q_lora, hds, QK_NOPE + QK_ROPE)
            wq = jnp.pad(wq, ((0, 0), (0, 0), (0, QK_PAD - QK_NOPE - QK_ROPE))).reshape(cfg.q_lora, hds * QK_PAD)
            q, k, vt = _mla_b(ckv, cq, kr, cos_t, sin_t, bf(w_uk), bf(w_uv.T), bf(wq.T), row(k_nope_norm),
                              q_nope_norm[j].reshape(-1, 1), q_rope_norm[j].reshape(-1, 1), cfg, t)
            o = _attn(q, k, vt, cfg, t)
            h = _matmul_res(o.reshape(m, hds * V_HEAD), bf(mla_out_w), j, h, t, "mla_out_proj")
        h = _mlp(h, row(ln_mlp[i]), mlp_w1, mlp_w2, i, t)
        h = _ple(h, row(ple_norm[i]), bf(ple_gate_w[i]), p2, i, bf(ple_proj_w[i]), t)
    return h.reshape(cfg.batch, cfg.seq, d)


_CFG = _Cfg(batch=2, seq=4096, d_model=2048, d_ff=8192, ple_dim=256, d_inner=4096, ssm_groups=8,
            mla_heads=16, q_lora=512, kv_lora=512)
_TILES = _Tiles(tm_proj=1024, tn_proj=512, tn_in=1024, tm_mlp=1024, tf_mlp=512, tm_ple=512, tm_mla=512, t_attn=512,
                attn_heads=4, vmem_mb=56)


def kernel(x, p, positions, ln_mix, ln_mlp, mlp_w1, mlp_w2, ple_norm, ple_gate_w, ple_proj_w, ssm_in_w, ssm_conv_w, ssm_conv_b, ssm_dt_bias, ssm_a_log, ssm_d, ssm_norm, ssm_out_w, kv_in_norm, w_dkv, ckv_norm, w_kr, w_uk, w_uv, k_nope_norm, k_rope_norm, w_dq, cq_norm, w_uq, q_nope_norm, q_rope_norm, mla_out_w):
    return _forward(_CFG, _TILES, x, p, positions, ln_mix, ln_mlp, mlp_w1, mlp_w2, ple_norm, ple_gate_w,
                    ple_proj_w, ssm_in_w, ssm_conv_w, ssm_conv_b, ssm_dt_bias, ssm_a_log, ssm_d, ssm_norm,
                    ssm_out_w, kv_in_norm, w_dkv, ckv_norm, w_kr, w_uk, w_uv, k_nope_norm, k_rope_norm,
                    w_dq, cq_norm, w_uq, q_nope_norm, q_rope_norm, mla_out_w)
```

```python
import functools
import math
from typing import NamedTuple

import jax
import jax.numpy as jnp
from jax import lax
from jax.experimental import pallas as pl
from jax.experimental.pallas import tpu as pltpu

F32 = jnp.float32
BF16 = jnp.bfloat16
EPS = 1e-6

LANES = 128
SUBLANES = 8
SSD_CHUNK = 128
HEADDIM = 64
SSM_STATE = 128
CONV_W = 4
CONV_HIST = 8
CONV_STRIDE = 4
MASK_CHUNK_SHIFT = 6
QK_NOPE = 128
QK_ROPE = 64
V_HEAD = 128
QK_PAD = 256
ROPE_THETA = 10000.0
LOG2E = math.log2(math.e)
MIB = 1024 * 1024


class _Cfg(NamedTuple):
    batch: int
    seq: int
    d_model: int
    d_ff: int
    ple_dim: int
    d_inner: int
    ssm_groups: int
    mla_heads: int
    q_lora: int
    kv_lora: int


class _Tiles(NamedTuple):
    tm_proj: int
    tm_out: int
    tn_proj: int
    tn_in: int
    tm_mlp: int
    tf_mlp: int
    tm_ple: int
    tm_mla: int
    t_attn: int
    attn_heads: int
    up_heads: int
    vmem_mb: int


def _cparams(sem, vmem_mb):
    return pltpu.CompilerParams(dimension_semantics=sem, vmem_limit_bytes=vmem_mb * MIB)


def _rms(x, g):
    return x * lax.rsqrt(jnp.mean(x * x, axis=-1, keepdims=True) + EPS) * g


def _sigmoid(x):
    return 1.0 / (1.0 + jnp.exp2(x * (-LOG2E)))


def _softplus(x):
    return jnp.maximum(x, 0.0) + jnp.log1p(jnp.exp(-jnp.abs(x)))


def _dot(a, b):
    return jnp.dot(a, b, preferred_element_type=F32)


def _dot_nt(a, b):
    return lax.dot_general(a, b, (((1,), (1,)), ((), ())), preferred_element_type=F32)


def _split3(v):
    hi = v.astype(BF16)
    r = v - hi.astype(F32)
    mid = r.astype(BF16)
    lo = (r - mid.astype(F32)).astype(BF16)
    return hi, mid, lo


def _inproj_kernel(h_ref, g_ref, w_ref, wdt_ref, zx_ref, dt_ref, hn_ref):
    @pl.when(pl.program_id(1) == 0)
    def _():
        hn = _rms(h_ref[...], g_ref[...]).astype(BF16)
        hn_ref[...] = hn
        dt_ref[...] = _dot_nt(hn, wdt_ref[...])

    zx_ref[...] = _dot_nt(hn_ref[...], w_ref[...].astype(BF16)).astype(zx_ref.dtype)


def _inproj(h, gain, w_in_t, layer, n, w_dt, t):
    m, k = h.shape
    tm, tn = t.tm_proj, t.tn_in
    return pl.pallas_call(
        _inproj_kernel,
        grid=(m // tm, n // tn),
        in_specs=[
            pl.BlockSpec((tm, k), lambda i, j: (i, 0)),
            pl.BlockSpec((1, k), lambda i, j: (0, 0)),
            pl.BlockSpec((None, tn, k), lambda i, j: (layer, j, 0)),
            pl.BlockSpec((LANES, k), lambda i, j: (0, 0)),
        ],
        out_specs=[
            pl.BlockSpec((tm, tn), lambda i, j: (i, j)),
            pl.BlockSpec((tm, LANES), lambda i, j: (i, 0)),
        ],
        out_shape=[jax.ShapeDtypeStruct((m, n), BF16), jax.ShapeDtypeStruct((m, LANES), F32)],
        scratch_shapes=[pltpu.VMEM((tm, k), BF16)],
        compiler_params=_cparams(("parallel", "arbitrary"), t.vmem_mb),
        name="ssm_in_proj",
    )(h, gain, w_in_t, w_dt)


def _ssd_kernel(z_ref, x_ref, b_ref, c_ref, dt_ref, cwb_ref, dtb_ref, alog_ref, dfull_ref, ng_ref, e_ref,
                out_ref, buf_ref, xc_ref, state_ref, *, d_inner, gn, groups):
    ln = SSD_CHUNK
    cdim = d_inner + 2 * gn
    nslab = cdim // LANES
    sl_b = d_inner // LANES
    sl_c = (d_inner + gn) // LANES
    heads = d_inner // HEADDIM
    hg = heads // groups
    gw = hg * HEADDIM
    c = pl.program_id(1)

    @pl.when(c == 0)
    def _():
        buf_ref[:, 0:CONV_HIST, :] = jnp.zeros((nslab, CONV_HIST, LANES), F32)
        state_ref[...] = jnp.zeros_like(state_ref)

    @pl.when(c > 0)
    def _():
        buf_ref[:, 0:CONV_HIST, :] = buf_ref[:, ln:ln + CONV_HIST, :]

    for s in range(nslab):
        lo = s * LANES
        if lo < d_inner:
            src = x_ref[:, lo:lo + LANES]
        elif lo < d_inner + gn:
            src = b_ref[:, lo - d_inner:lo - d_inner + LANES]
        else:
            src = c_ref[:, lo - d_inner - gn:lo - d_inner - gn + LANES]
        buf_ref[s, CONV_HIST:CONV_HIST + ln, :] = src.astype(F32)

    rows_per_blk = SUBLANES * CONV_STRIDE

    def conv_slab(s, carry):
        wb = cwb_ref[s]
        for blk in range(ln // rows_per_blk):
            for i in range(CONV_STRIDE):
                t0 = blk * rows_per_blk + i
                acc = wb[CONV_W:CONV_W + 1, :]
                for k in range(CONV_W):
                    start = CONV_HIST + t0 - (CONV_W - 1) + k
                    acc = acc + wb[k:k + 1, :] * buf_ref[s, pl.ds(start, SUBLANES, stride=CONV_STRIDE), :]
                xc_ref[s, pl.ds(t0, SUBLANES, stride=CONV_STRIDE), :] = acc * _sigmoid(acc)
        return carry

    lax.fori_loop(0, nslab, conv_slab, 0, unroll=8)

    dt = _softplus(dt_ref[...] + dtb_ref[...])
    a = dt * (-jnp.exp(alog_ref[...])) * LOG2E
    row = lax.broadcasted_iota(jnp.int32, (ln, ln), 0)
    col = lax.broadcasted_iota(jnp.int32, (ln, ln), 1)
    causal = row >= col
    tril = jnp.where(causal, 1.0, 0.0).astype(BF16)
    a_hi, a_mid, a_lo = _split3(a)
    acs = _dot(tril, a_hi) + _dot(tril, a_mid) + _dot(tril, a_lo)
    acs_t = acs.T
    last = acs[ln - 1:ln, :]
    eacs = jnp.exp2(acs)
    wend = jnp.exp2(last - acs) * dt
    lo_half = lax.broadcasted_iota(jnp.int32, (ln, LANES), 1) < HEADDIM

    def hi_mid(v):
        hi = v.astype(BF16)
        return jnp.concatenate([hi, (v - hi.astype(F32)).astype(BF16)], axis=1)

    per_head = jnp.concatenate([hi_mid(eacs), hi_mid(wend), hi_mid(dt)], axis=0)

    for g in range(groups):
        gcols = slice(g * gw, (g + 1) * gw)
        bg = xc_ref[sl_b + g]
        cg = xc_ref[sl_c + g].astype(BF16)
        cb = _dot_nt(cg, bg.astype(BF16))
        bt = bg.T.astype(BF16)
        eg = e_ref[:, gcols]
        xg = jnp.concatenate([xc_ref[g * (gw // LANES) + q] for q in range(gw // LANES)], axis=1)
        st = state_ref[:, gcols]
        lanes_g = _dot(per_head, eg)
        eacs_g = lanes_g[0:ln]
        y_off = _dot(cg, st.astype(BF16)) * eacs_g
        s_new = _dot(bt, (xg * lanes_g[ln:2 * ln]).astype(BF16))
        state_ref[:, gcols] = st * eacs_g[ln - 1:ln, :] + s_new
        xdt = xg * lanes_g[2 * ln:3 * ln]
        ys = []
        for p in range(hg // 2):
            pr = (g * hg) // 2 + p
            m_parts = []
            for hh in (2 * pr, 2 * pr + 1):
                a_row = acs_t[hh:hh + 1, :]
                a_col = jnp.broadcast_to(acs[:, hh:hh + 1], (ln, ln))
                dec = jnp.exp2(jnp.where(causal, a_col - a_row, -jnp.inf))
                m_parts.append((cb * dec).astype(BF16))
            x2 = xdt[:, p * LANES:(p + 1) * LANES]
            xa = jnp.where(lo_half, x2, 0.0).astype(BF16)
            xb = jnp.where(lo_half, 0.0, x2).astype(BF16)
            ys.append(_dot(jnp.concatenate(m_parts, axis=1), jnp.concatenate([xa, xb], axis=0)))
        yg = jnp.concatenate(ys, axis=1) + y_off + dfull_ref[:, gcols] * xg
        zg = z_ref[:, gcols].astype(F32)
        yg = yg * (zg * _sigmoid(zg))
        out_ref[:, gcols] = _rms(yg, ng_ref[:, gcols]).astype(out_ref.dtype)


def _ssd(zx, dt_raw, conv_wb, dt_bias, a_log, d_full, norm_g, cfg, t):
    ln = SSD_CHUNK
    d_inner = cfg.d_inner
    gn = cfg.ssm_groups * SSM_STATE
    cdim = d_inner + 2 * gn
    nslab = cdim // LANES
    nc = cfg.seq // ln
    m = cfg.batch * cfg.seq
    b_blk = (2 * d_inner) // gn
    row = lambda b, c: b * nc + c
    kern = functools.partial(_ssd_kernel, d_inner=d_inner, gn=gn, groups=cfg.ssm_groups)
    head_lanes = (jnp.arange(d_inner)[None, :] // HEADDIM == jnp.arange(LANES)[:, None]).astype(BF16)
    head_lanes = jnp.concatenate([head_lanes, head_lanes], axis=0)
    const = lambda b, c: (0, 0)
    return pl.pallas_call(
        kern,
        grid=(cfg.batch, nc),
        in_specs=[
            pl.BlockSpec((ln, d_inner), lambda b, c: (row(b, c), 0)),
            pl.BlockSpec((ln, d_inner), lambda b, c: (row(b, c), 1)),
            pl.BlockSpec((ln, gn), lambda b, c: (row(b, c), b_blk)),
            pl.BlockSpec((ln, gn), lambda b, c: (row(b, c), b_blk + 1)),
            pl.BlockSpec((ln, LANES), lambda b, c: (row(b, c), 0)),
            pl.BlockSpec((nslab, SUBLANES, LANES), lambda b, c: (0, 0, 0)),
            pl.BlockSpec((1, LANES), const),
            pl.BlockSpec((1, LANES), const),
            pl.BlockSpec((1, d_inner), const),
            pl.BlockSpec((1, d_inner), const),
            pl.BlockSpec((2 * LANES, d_inner), const),
        ],
        out_specs=pl.BlockSpec((ln, d_inner), lambda b, c: (row(b, c), 0)),
        out_shape=jax.ShapeDtypeStruct((m, d_inner), BF16),
        scratch_shapes=[
            pltpu.VMEM((nslab, CONV_HIST + ln, LANES), F32),
            pltpu.VMEM((nslab, ln, LANES), F32),
            pltpu.VMEM((SSM_STATE, d_inner), F32),
        ],
        compiler_params=_cparams(("parallel", "arbitrary"), t.vmem_mb),
        name="ssm_scan",
    )(zx, zx, zx, zx, dt_raw, conv_wb, dt_bias, a_log, d_full, norm_g, head_lanes)


def _matmul_res_kernel(a_ref, w_ref, r_ref, o_ref, *, tn):
    a = a_ref[...]
    for j in range(o_ref.shape[1] // tn):
        cols = slice(j * tn, (j + 1) * tn)
        o_ref[:, cols] = r_ref[:, cols] + _dot(a, w_ref[:, cols])


def _matmul_res(a, w, layer, res, t, name):
    m, k = a.shape
    n = w.shape[2]
    tm, tn = t.tm_out, min(t.tn_proj, n)
    return pl.pallas_call(
        functools.partial(_matmul_res_kernel, tn=tn),
        grid=(m // tm,),
        in_specs=[
            pl.BlockSpec((tm, k), lambda i: (i, 0)),
            pl.BlockSpec((None, k, n), lambda i: (layer, 0, 0), pipeline_mode=pl.Buffered(1)),
            pl.BlockSpec((tm, n), lambda i: (i, 0)),
        ],
        out_specs=pl.BlockSpec((tm, n), lambda i: (i, 0)),
        out_shape=jax.ShapeDtypeStruct((m, n), F32),
        compiler_params=_cparams(("parallel",), t.vmem_mb),
        name=name,
    )(a, w, res)


def _mlp_kernel(h_ref, g_ref, w1_ref, w2_ref, o_ref, hn_ref):
    @pl.when(pl.program_id(1) == 0)
    def _():
        h = h_ref[...]
        hn_ref[...] = _rms(h, g_ref[...]).astype(BF16)
        o_ref[...] = h

    a = jnp.maximum(_dot(hn_ref[...], w1_ref[...].astype(BF16)), 0.0)
    o_ref[...] += _dot((a * a).astype(BF16), w2_ref[...].astype(BF16))


def _mlp(h, gain, w1, w2, layer, t):
    m, d = h.shape
    f = w1.shape[2]
    tm, tf = t.tm_mlp, t.tf_mlp
    return pl.pallas_call(
        _mlp_kernel,
        grid=(m // tm, f // tf),
        in_specs=[
            pl.BlockSpec((tm, d), lambda i, j: (i, 0)),
            pl.BlockSpec((1, d), lambda i, j: (0, 0)),
            pl.BlockSpec((None, d, tf), lambda i, j: (layer, 0, j)),
            pl.BlockSpec((None, tf, d), lambda i, j: (layer, j, 0)),
        ],
        out_specs=pl.BlockSpec((tm, d), lambda i, j: (i, 0)),
        out_shape=jax.ShapeDtypeStruct((m, d), F32),
        scratch_shapes=[pltpu.VMEM((tm, d), BF16)],
        compiler_params=_cparams(("parallel", "arbitrary"), t.vmem_mb),
        name="sq_relu_mlp",
    )(h, gain, w1, w2)


def _ple_kernel(h_ref, g_ref, wg_ref, p_ref, wp_ref, o_ref, *, tn):
    hn = _rms(h_ref[...], g_ref[...]).astype(BF16)
    pb = p_ref[...].astype(BF16)
    for j in range(h_ref.shape[1] // tn):
        cols = slice(j * tn, (j + 1) * tn)
        gate = _sigmoid(_dot(hn, wg_ref[:, cols]))
        o_ref[:, cols] = h_ref[:, cols] + gate * _dot(pb, wp_ref[:, cols])


def _ple(h, gain, wg, p, layer, wp, t):
    m, d = h.shape
    pd = p.shape[2]
    tm = t.tm_ple
    tn = 512 if d % 512 == 0 else LANES
    return pl.pallas_call(
        functools.partial(_ple_kernel, tn=tn),
        grid=(m // tm,),
        in_specs=[
            pl.BlockSpec((tm, d), lambda i: (i, 0)),
            pl.BlockSpec((1, d), lambda i: (0, 0)),
            pl.BlockSpec((d, d), lambda i: (0, 0)),
            pl.BlockSpec((None, tm, pd), lambda i: (layer, i, 0)),
            pl.BlockSpec((pd, d), lambda i: (0, 0)),
        ],
        out_specs=pl.BlockSpec((tm, d), lambda i: (i, 0)),
        out_shape=jax.ShapeDtypeStruct((m, d), F32),
        compiler_params=_cparams(("parallel",), t.vmem_mb),
        name="per_layer_input",
    )(h, gain, wg, p, wp)


def _rope(x, cos_f, sin_s):
    lane = lax.broadcasted_iota(jnp.int32, x.shape, 1)
    half = QK_ROPE // 2
    rot = jnp.where(lane < half, pltpu.roll(x, LANES - half, 1), pltpu.roll(x, half, 1))
    return x * cos_f + rot * sin_s


def _rms_rope_part(x, g_pad):
    ms = jnp.sum(x * x, axis=-1, keepdims=True) * (1.0 / QK_ROPE)
    return x * lax.rsqrt(ms + EPS) * g_pad


def _mla_a_kernel(h_ref, gkv_ref, gq_ref, wdkv_ref, wkr_ref, wdq_ref, ckvn_ref, krn_ref, cqn_ref,
                  pos_ref, fr_ref, ckv_ref, cq_ref, kr_ref, cos_ref, sin_ref):
    h = h_ref[...]
    y = h * lax.rsqrt(jnp.mean(h * h, axis=-1, keepdims=True) + EPS)
    hkv = (y * gkv_ref[...]).astype(BF16)
    hq = (y * gq_ref[...]).astype(BF16)
    ckv_ref[...] = _rms(_dot(hkv, wdkv_ref[...]), ckvn_ref[...]).astype(BF16)
    cq_ref[...] = _rms(_dot(hq, wdq_ref[...]), cqn_ref[...]).astype(BF16)
    ang = fr_ref[...] * pos_ref[...].astype(F32)
    cos_t = jnp.cos(ang)
    sin_t = jnp.sin(ang)
    cos_ref[...] = cos_t
    sin_ref[...] = sin_t
    pad = jnp.zeros((LANES - QK_ROPE, ang.shape[1]), F32)
    cos_f = jnp.concatenate([cos_t, cos_t, pad], axis=0).T
    sin_s = jnp.concatenate([-sin_t, sin_t, pad], axis=0).T
    kr = _rms_rope_part(_dot(hkv, wkr_ref[...]), krn_ref[...])
    kr_ref[...] = _rope(kr, cos_f, sin_s).astype(BF16)


def _mla_a(h, g_kv, g_q, w_dkv, w_kr, w_dq, ckv_n, kr_n, cq_n, pos, freqs, t):
    m, d = h.shape
    kvl = w_dkv.shape[1]
    ql = w_dq.shape[1]
    tm = t.tm_mla
    rowblk = lambda w: pl.BlockSpec((tm, w), lambda i: (i, 0))
    full = lambda a: pl.BlockSpec(a.shape, lambda i: (0, 0))
    colblk = pl.BlockSpec((QK_ROPE // 2, tm), lambda i: (0, i))
    return pl.pallas_call(
        _mla_a_kernel,
        grid=(m // tm,),
        in_specs=[rowblk(d), full(g_kv), full(g_q), full(w_dkv), full(w_kr), full(w_dq), full(ckv_n),
                  full(kr_n), full(cq_n), pl.BlockSpec((1, tm), lambda i: (0, i)), full(freqs)],
        out_specs=[rowblk(kvl), rowblk(ql), rowblk(LANES), colblk, colblk],
        out_shape=[jax.ShapeDtypeStruct((m, kvl), BF16), jax.ShapeDtypeStruct((m, ql), BF16),
                   jax.ShapeDtypeStruct((m, LANES), BF16), jax.ShapeDtypeStruct((QK_ROPE // 2, m), F32),
                   jax.ShapeDtypeStruct((QK_ROPE // 2, m), F32)],
        compiler_params=_cparams(("parallel",), t.vmem_mb),
        name="mla_down_proj",
    )(h, g_kv, g_q, w_dkv, w_kr, w_dq, ckv_n, kr_n, cq_n, pos, freqs)


def _mla_b_kernel(ckv_ref, cq_ref, kr_ref, cos_ref, sin_ref, wuk_ref, wuvt_ref, wuqt_ref, knn_ref, qnn_ref,
                  qrn_ref, q_ref, k_ref, vt_ref, *, scale, heads, hblk):
    ckv = ckv_ref[...]
    cq = cq_ref[...]
    kr = kr_ref[...]
    cos_t = cos_ref[...]
    sin_t = sin_ref[...]
    half = QK_ROPE // 2
    g_nope = qnn_ref[...] * scale
    g_rope = qrn_ref[...] * scale
    tm = ckv.shape[0]
    for h0 in range(0, heads, hblk):
        kn_blk = _dot(ckv, wuk_ref[:, h0 * QK_NOPE:(h0 + hblk) * QK_NOPE])
        vt_blk = _dot_nt(wuvt_ref[h0 * V_HEAD:(h0 + hblk) * V_HEAD, :], ckv)
        qt_blk = _dot_nt(wuqt_ref[h0 * QK_PAD:(h0 + hblk) * QK_PAD, :], cq)
        for i in range(hblk):
            h = h0 + i
            kn = _rms(kn_blk[:, i * QK_NOPE:(i + 1) * QK_NOPE], knn_ref[...])
            k_ref[0, h, :, 0:QK_NOPE] = kn.astype(BF16)
            k_ref[0, h, :, QK_NOPE:QK_PAD] = kr
            vt_ref[0, h, 0] = vt_blk[i * V_HEAD:(i + 1) * V_HEAD].astype(BF16)
            qt = qt_blk[i * QK_PAD:(i + 1) * QK_PAD]
            qn = qt[0:QK_NOPE]
            qn = qn * lax.rsqrt(jnp.mean(qn * qn, axis=0, keepdims=True) + EPS) * g_nope
            qr = qt[QK_NOPE:QK_NOPE + QK_ROPE]
            qr = qr * lax.rsqrt(jnp.mean(qr * qr, axis=0, keepdims=True) + EPS) * g_rope
            x1 = qr[0:half]
            x2 = qr[half:QK_ROPE]
            q_ref[0, h, 0, 0:QK_NOPE, :] = qn.astype(BF16)
            q_ref[0, h, 0, QK_NOPE:QK_NOPE + half, :] = (x1 * cos_t - x2 * sin_t).astype(BF16)
            q_ref[0, h, 0, QK_NOPE + half:QK_NOPE + QK_ROPE, :] = (x2 * cos_t + x1 * sin_t).astype(BF16)
            q_ref[0, h, 0, QK_NOPE + QK_ROPE:QK_PAD, :] = jnp.zeros((QK_PAD - QK_NOPE - QK_ROPE, tm), BF16)


def _mla_b(ckv, cq, kr, cos_t, sin_t, w_uk, w_uvt, w_uqt, kn_n, qn_n, qr_n, cfg, t):
    m, kvl = ckv.shape
    ql = cq.shape[1]
    hds = cfg.mla_heads
    tm = t.t_attn
    ns = cfg.seq // tm
    scale = (QK_NOPE + QK_ROPE) ** -0.5 * LOG2E
    rowblk = lambda w: pl.BlockSpec((tm, w), lambda i: (i, 0))
    full = lambda a: pl.BlockSpec(a.shape, lambda i: (0, 0))
    colblk = pl.BlockSpec((QK_ROPE // 2, tm), lambda i: (0, i))
    return pl.pallas_call(
        functools.partial(_mla_b_kernel, scale=scale, heads=hds, hblk=t.up_heads),
        grid=(m // tm,),
        in_specs=[rowblk(kvl), rowblk(ql), rowblk(LANES), colblk, colblk,
                  full(w_uk), full(w_uvt), full(w_uqt), full(kn_n), full(qn_n), full(qr_n)],
        out_specs=[
            pl.BlockSpec((1, hds, 1, QK_PAD, tm), lambda i: (i // ns, 0, i % ns, 0, 0)),
            pl.BlockSpec((1, hds, tm, QK_PAD), lambda i: (i // ns, 0, i % ns, 0)),
            pl.BlockSpec((1, hds, 1, V_HEAD, tm), lambda i: (i // ns, 0, i % ns, 0, 0)),
        ],
        out_shape=[jax.ShapeDtypeStruct((cfg.batch, hds, ns, QK_PAD, tm), BF16),
                   jax.ShapeDtypeStruct((cfg.batch, hds, cfg.seq, QK_PAD), BF16),
                   jax.ShapeDtypeStruct((cfg.batch, hds, ns, V_HEAD, tm), BF16)],
        compiler_params=_cparams(("parallel",), t.vmem_mb),
        name="mla_up_proj",
    )(ckv, cq, kr, cos_t, sin_t, w_uk, w_uvt, w_uqt, kn_n, qn_n, qr_n)


def _attn_kernel(q_ref, k_ref, vt_ref, o_ref, m_ref, l_ref, acc_ref, sa_ref, sb_ref, bma_ref, bmb_ref, *, tq, hb):
    qi = pl.program_id(2)
    tk = tq // 2
    m_ref[...] = jnp.full_like(m_ref, -jnp.inf)
    l_ref[...] = jnp.zeros_like(l_ref)
    acc_ref[...] = jnp.zeros_like(acc_ref)

    def visible_mask(k0):
        krow = k0 + lax.broadcasted_iota(jnp.int32, (tk, tq), 0)
        qcol = qi * tq + lax.broadcasted_iota(jnp.int32, (tk, tq), 1)
        return (krow >> MASK_CHUNK_SHIFT) <= (qcol >> MASK_CHUNK_SHIFT)

    def qk_stage(j, half, s_ref, bm_ref, masked):
        k0 = j * tq + half * tk
        if masked:
            visible = visible_mask(k0)
        for hh in range(hb):
            k = k_ref[0, hh, pl.ds(pl.multiple_of(k0, tk), tk), :]
            s = _dot(k, q_ref[0, hh, 0])
            if masked:
                s = jnp.where(visible, s, -jnp.inf)
            s_ref[hh] = s
            bm_ref[hh] = jnp.max(s, axis=0, keepdims=True)

    def sm_stage(j, half, s_ref, bm_ref, masked=False):
        if masked:
            visible = visible_mask(j * tq + half * tk)
        for hh in range(hb):
            m_prev = m_ref[hh]
            if masked:
                s = jnp.where(visible, s_ref[hh], -jnp.inf)
                m_new = jnp.maximum(m_prev, jnp.max(s, axis=0, keepdims=True))
            else:
                s = s_ref[hh]
                m_new = jnp.maximum(m_prev, bm_ref[hh])
            p = jnp.exp2(s - m_new)
            alpha = jnp.exp2(m_prev - m_new)
            l_ref[hh] = alpha * l_ref[hh] + jnp.sum(p, axis=0, keepdims=True)
            vt = vt_ref[0, hh, j, :, half * tk:(half + 1) * tk]
            acc_ref[hh] = acc_ref[hh] * alpha + _dot(vt, p.astype(BF16))
            m_ref[hh] = m_new

    qk_stage(0, 0, sa_ref, bma_ref, False)

    def body(j, carry):
        qk_stage(j, 1, sb_ref, bmb_ref, False)
        sm_stage(j, 0, sa_ref, bma_ref)
        qk_stage(j + 1, 0, sa_ref, bma_ref, False)
        sm_stage(j, 1, sb_ref, bmb_ref)
        return carry

    lax.fori_loop(0, qi, body, 0)
    k1 = qi * tq + tk
    hi_q = slice(tk, tq)
    krow = k1 + lax.broadcasted_iota(jnp.int32, (tk, tk), 0)
    qcol = k1 + lax.broadcasted_iota(jnp.int32, (tk, tk), 1)
    visible_q = (krow >> MASK_CHUNK_SHIFT) <= (qcol >> MASK_CHUNK_SHIFT)
    for hh in range(hb):
        k = k_ref[0, hh, pl.ds(pl.multiple_of(k1, tk), tk), :]
        s = _dot(k, q_ref[0, hh, 0, :, hi_q])
        sb_ref[hh, :, hi_q] = jnp.where(visible_q, s, -jnp.inf)
    sm_stage(qi, 0, sa_ref, bma_ref, masked=True)
    for hh in range(hb):
        s = sb_ref[hh, :, hi_q]
        m_prev = m_ref[hh, :, hi_q]
        m_new = jnp.maximum(m_prev, jnp.max(s, axis=0, keepdims=True))
        p = jnp.exp2(s - m_new)
        alpha = jnp.exp2(m_prev - m_new)
        l_ref[hh, :, hi_q] = alpha * l_ref[hh, :, hi_q] + jnp.sum(p, axis=0, keepdims=True)
        vt = vt_ref[0, hh, qi, :, hi_q]
        acc_ref[hh, :, hi_q] = acc_ref[hh, :, hi_q] * alpha + _dot(vt, p.astype(BF16))
        m_ref[hh, :, hi_q] = m_new
    for hh in range(hb):
        o = acc_ref[hh] / l_ref[hh]
        o_ref[0, :, hh * V_HEAD:(hh + 1) * V_HEAD] = o.T.astype(o_ref.dtype)


def _attn(q, k, vt, cfg, t):
    tq = t.t_attn
    hb = t.attn_heads
    hds = cfg.mla_heads
    nq = cfg.seq // tq
    return pl.pallas_call(
        functools.partial(_attn_kernel, tq=tq, hb=hb),
        grid=(cfg.batch, hds // hb, nq),
        in_specs=[
            pl.BlockSpec((1, hb, 1, QK_PAD, tq), lambda b, h, i: (b, h, i, 0, 0)),
            pl.BlockSpec((1, hb, cfg.seq, QK_PAD), lambda b, h, i: (b, h, 0, 0)),
            pl.BlockSpec((1, hb, nq, V_HEAD, tq), lambda b, h, i: (b, h, 0, 0, 0)),
        ],
        out_specs=pl.BlockSpec((1, tq, hb * V_HEAD), lambda b, h, i: (b, i, h)),
        out_shape=jax.ShapeDtypeStruct((cfg.batch, cfg.seq, hds * V_HEAD), BF16),
        scratch_shapes=[pltpu.VMEM((hb, 1, tq), F32), pltpu.VMEM((hb, 1, tq), F32),
                        pltpu.VMEM((hb, V_HEAD, tq), F32),
                        pltpu.VMEM((hb, tq // 2, tq), F32), pltpu.VMEM((hb, tq // 2, tq), F32),
                        pltpu.VMEM((hb, 1, tq), F32), pltpu.VMEM((hb, 1, tq), F32)],
        compiler_params=_cparams(("parallel", "parallel", "arbitrary"), t.vmem_mb),
        name="mla_flash_attn",
    )(q, k, vt)


def _pad_lanes(v, width=LANES):
    v = v.reshape(1, -1)
    return jnp.pad(v, ((0, 0), (0, width - v.shape[1])))


def _forward(cfg, t, x, p, positions, ln_mix, ln_mlp, mlp_w1, mlp_w2, ple_norm, ple_gate_w, ple_proj_w,
             ssm_in_w, ssm_conv_w, ssm_conv_b, ssm_dt_bias, ssm_a_log, ssm_d, ssm_norm, ssm_out_w,
             kv_in_norm, w_dkv, ckv_norm, w_kr, w_uk, w_uv, k_nope_norm, k_rope_norm,
             w_dq, cq_norm, w_uq, q_nope_norm, q_rope_norm, mla_out_w):
    depth = ln_mix.shape[0]
    n_a = ssm_in_w.shape[0]
    m = cfg.batch * cfg.seq
    d = cfg.d_model
    hds = cfg.mla_heads
    row = lambda v: v.reshape(1, -1).astype(F32)
    bf = lambda w: w.astype(BF16)
    h = x.reshape(m, d)
    p2 = p.reshape(depth, m, cfg.ple_dim)

    kv_ready = False
    for i in range(depth):
        if i < n_a:
            j = i
            d_inner = cfg.d_inner
            cdim = d_inner + 2 * cfg.ssm_groups * SSM_STATE
            w_in_t = jnp.swapaxes(ssm_in_w, 1, 2)
            w_dt = bf(jnp.pad(w_in_t[j, d_inner + cdim:], ((0, LANES - d_inner // HEADDIM), (0, 0))))
            zx, dt_raw = _inproj(h, row(ln_mix[i]), w_in_t, j, d_inner + cdim, w_dt, t)
            cwb = jnp.concatenate([ssm_conv_w[j], ssm_conv_b[j][None, :],
                                   jnp.zeros((SUBLANES - CONV_W - 1, cdim), F32)], axis=0)
            cwb = cwb.reshape(SUBLANES, cdim // LANES, LANES).transpose(1, 0, 2)
            yn = _ssd(zx, dt_raw, cwb, _pad_lanes(ssm_dt_bias[j]), _pad_lanes(ssm_a_log[j]),
                      row(jnp.repeat(ssm_d[j], HEADDIM)), row(ssm_norm[j]), cfg, t)
            h = _matmul_res(yn, bf(ssm_out_w), j, h, t, "ssm_out_proj")
        else:
            j = i - n_a
            if not kv_ready:
                kv_h = h
                kv_ready = True
            half = QK_ROPE // 2
            fr = ROPE_THETA ** (-jnp.arange(half, dtype=F32) / half)
            freqs = fr.reshape(half, 1)
            w_kr_p = bf(jnp.pad(w_kr, ((0, 0), (0, LANES - QK_ROPE))))
            assert depth - n_a == 1
            ckv, cq, kr, cos_t, sin_t = _mla_a(
                kv_h, row(kv_in_norm), row(ln_mix[i]), bf(w_dkv), w_kr_p, bf(w_dq[j]), row(ckv_norm),
                _pad_lanes(k_rope_norm), row(cq_norm[j]), positions.reshape(1, m), freqs, t)
            wq = w_uq[j].reshape(cfg.q_lora, hds, QK_NOPE + QK_ROPE)
            wq = jnp.pad(wq, ((0, 0), (0, 0), (0, QK_PAD - QK_NOPE - QK_ROPE))).reshape(cfg.q_lora, hds * QK_PAD)
            q, k, vt = _mla_b(ckv, cq, kr, cos_t, sin_t, bf(w_uk), bf(w_uv.T), bf(wq.T), row(k_nope_norm),
                              q_nope_norm[j].reshape(-1, 1), q_rope_norm[j].reshape(-1, 1), cfg, t)
            o = _attn(q, k, vt, cfg, t)
            h = _matmul_res(o.reshape(m, hds * V_HEAD), bf(mla_out_w), j, h, t, "mla_out_proj")
        h = _mlp(h, row(ln_mlp[i]), mlp_w1, mlp_w2, i, t)
        h = _ple(h, row(ple_norm[i]), bf(ple_gate_w[i]), p2, i, bf(ple_proj_w[i]), t)
    return h.reshape(cfg.batch, cfg.seq, d)


_CFG = _Cfg(batch=2, seq=4096, d_model=2048, d_ff=8192, ple_dim=256, d_inner=4096, ssm_groups=8,
            mla_heads=16, q_lora=512, kv_lora=512)
_TILES = _Tiles(tm_proj=1024, tm_out=512, tn_proj=512, tn_in=1024, tm_mlp=1024, tf_mlp=512, tm_ple=512, tm_mla=512, t_attn=512,
                attn_heads=4, up_heads=2, vmem_mb=56)


def kernel(x, p, positions, ln_mix, ln_mlp, mlp_w1, mlp_w2, ple_norm, ple_gate_w, ple_proj_w, ssm_in_w, ssm_conv_w, ssm_conv_b, ssm_dt_bias, ssm_a_log, ssm_d, ssm_norm, ssm_out_w, kv_in_norm, w_dkv, ckv_norm, w_kr, w_uk, w_uv, k_nope_norm, k_rope_norm, w_dq, cq_norm, w_uq, q_nope_norm, q_rope_norm, mla_out_w):
    return _forward(_CFG, _TILES, x, p, positions, ln_mix, ln_mlp, mlp_w1, mlp_w2, ple_norm, ple_gate_w,
                    ple_proj_w, ssm_in_w, ssm_conv_w, ssm_conv_b, ssm_dt_bias, ssm_a_log, ssm_d, ssm_norm,
                    ssm_out_w, kv_in_norm, w_dkv, ckv_norm, w_kr, w_uk, w_uv, k_nope_norm, k_rope_norm,
                    w_dq, cq_norm, w_uq, q_nope_norm, q_rope_norm, mla_out_w)
```

```python
import functools
import math
from typing import NamedTuple

import jax
import jax.numpy as jnp
from jax import lax
from jax.experimental import pallas as pl
from jax.experimental.pallas import tpu as pltpu

F32 = jnp.float32
BF16 = jnp.bfloat16
EPS = 1e-6

LANES = 128
SUBLANES = 8
SSD_CHUNK = 128
HEADDIM = 64
SSM_STATE = 128
CONV_W = 4
CONV_HIST = 8
CONV_STRIDE = 4
ROW_CHUNK = 256
MASK_CHUNK_SHIFT = 6
QK_NOPE = 128
QK_ROPE = 64
V_HEAD = 128
QK_PAD = 256
V_AUG = 144
ROPE_THETA = 10000.0
LOG2E = math.log2(math.e)
MIB = 1024 * 1024


class _Cfg(NamedTuple):
    batch: int
    seq: int
    d_model: int
    d_ff: int
    ple_dim: int
    d_inner: int
    ssm_groups: int
    mla_heads: int
    q_lora: int
    kv_lora: int


class _Tiles(NamedTuple):
    tm_proj: int
    tm_out: int
    tn_proj: int
    tn_in: int
    tm_mlp: int
    tf_mlp: int
    tm_ple: int
    tm_mla: int
    t_attn: int
    attn_heads: int
    up_heads: int
    vmem_mb: int


def _cparams(sem, vmem_mb):
    return pltpu.CompilerParams(dimension_semantics=sem, vmem_limit_bytes=vmem_mb * MIB)


def _rms(x, g):
    return x * lax.rsqrt(jnp.mean(x * x, axis=-1, keepdims=True) + EPS) * g


def _sigmoid(x):
    return 1.0 / (1.0 + jnp.exp2(x * (-LOG2E)))


def _softplus(x):
    return jnp.maximum(x, 0.0) + jnp.log1p(jnp.exp(-jnp.abs(x)))


def _dot(a, b):
    return jnp.dot(a, b, preferred_element_type=F32)


def _dot_nt(a, b):
    return lax.dot_general(a, b, (((1,), (1,)), ((), ())), preferred_element_type=F32)


def _split3(v):
    hi = v.astype(BF16)
    r = v - hi.astype(F32)
    mid = r.astype(BF16)
    lo = (r - mid.astype(F32)).astype(BF16)
    return hi, mid, lo


def _inproj_kernel(h_ref, g_ref, w_ref, wdt_ref, zx_ref, dt_ref, hn_ref, *, rc):
    @pl.when(pl.program_id(1) == 0)
    def _():
        w = w_ref[...].astype(BF16)
        for r in range(h_ref.shape[0] // rc):
            rows = slice(r * rc, (r + 1) * rc)
            hn = _rms(h_ref[rows, :], g_ref[...]).astype(BF16)
            hn_ref[rows, :] = hn
            dt_ref[rows, :] = _dot_nt(hn, wdt_ref[...])
            zx_ref[rows, :] = _dot_nt(hn, w).astype(zx_ref.dtype)

    @pl.when(pl.program_id(1) > 0)
    def _():
        zx_ref[...] = _dot_nt(hn_ref[...], w_ref[...].astype(BF16)).astype(zx_ref.dtype)


def _inproj(h, gain, w_in_t, layer, n, w_dt, t):
    m, k = h.shape
    tm, tn = t.tm_proj, t.tn_in
    return pl.pallas_call(
        functools.partial(_inproj_kernel, rc=min(ROW_CHUNK, tm)),
        grid=(m // tm, n // tn),
        in_specs=[
            pl.BlockSpec((tm, k), lambda i, j: (i, 0)),
            pl.BlockSpec((1, k), lambda i, j: (0, 0)),
            pl.BlockSpec((None, tn, k), lambda i, j: (layer, j, 0)),
            pl.BlockSpec((LANES, k), lambda i, j: (0, 0)),
        ],
        out_specs=[
            pl.BlockSpec((tm, tn), lambda i, j: (i, j)),
            pl.BlockSpec((tm, LANES), lambda i, j: (i, 0)),
        ],
        out_shape=[jax.ShapeDtypeStruct((m, n), BF16), jax.ShapeDtypeStruct((m, LANES), F32)],
        scratch_shapes=[pltpu.VMEM((tm, k), BF16)],
        compiler_params=_cparams(("parallel", "arbitrary"), t.vmem_mb),
        name="ssm_in_proj",
    )(h, gain, w_in_t, w_dt)


def _ssd_kernel(z_ref, x_ref, b_ref, c_ref, dt_ref, cwb_ref, dtb_ref, alog_ref, dfull_ref, ng_ref, e_ref,
                out_ref, buf_ref, xc_ref, state_ref, *, d_inner, gn, groups):
    ln = SSD_CHUNK
    cdim = d_inner + 2 * gn
    nslab = cdim // LANES
    sl_b = d_inner // LANES
    sl_c = (d_inner + gn) // LANES
    heads = d_inner // HEADDIM
    hg = heads // groups
    gw = hg * HEADDIM
    c = pl.program_id(1)

    @pl.when(c == 0)
    def _():
        buf_ref[:, 0:CONV_HIST, :] = jnp.zeros((nslab, CONV_HIST, LANES), F32)
        state_ref[...] = jnp.zeros_like(state_ref)

    @pl.when(c > 0)
    def _():
        buf_ref[:, 0:CONV_HIST, :] = buf_ref[:, ln:ln + CONV_HIST, :]

    for s in range(nslab):
        lo = s * LANES
        if lo < d_inner:
            src = x_ref[:, lo:lo + LANES]
        elif lo < d_inner + gn:
            src = b_ref[:, lo - d_inner:lo - d_inner + LANES]
        else:
            src = c_ref[:, lo - d_inner - gn:lo - d_inner - gn + LANES]
        buf_ref[s, CONV_HIST:CONV_HIST + ln, :] = src.astype(F32)

    rows_per_blk = SUBLANES * CONV_STRIDE

    def conv_slab(s, carry):
        wb = cwb_ref[s]
        for blk in range(ln // rows_per_blk):
            for i in range(CONV_STRIDE):
                t0 = blk * rows_per_blk + i
                acc = wb[CONV_W:CONV_W + 1, :]
                for k in range(CONV_W):
                    start = CONV_HIST + t0 - (CONV_W - 1) + k
                    acc = acc + wb[k:k + 1, :] * buf_ref[s, pl.ds(start, SUBLANES, stride=CONV_STRIDE), :]
                xc_ref[s, pl.ds(t0, SUBLANES, stride=CONV_STRIDE), :] = acc * _sigmoid(acc)
        return carry

    lax.fori_loop(0, nslab, conv_slab, 0, unroll=8)

    dt = _softplus(dt_ref[...] + dtb_ref[...])
    a = dt * (-jnp.exp(alog_ref[...])) * LOG2E
    row = lax.broadcasted_iota(jnp.int32, (ln, ln), 0)
    col = lax.broadcasted_iota(jnp.int32, (ln, ln), 1)
    causal = row >= col
    tril = jnp.where(causal, 1.0, 0.0).astype(BF16)
    a_hi, a_mid, a_lo = _split3(a)
    acs = _dot(tril, a_hi) + _dot(tril, a_mid) + _dot(tril, a_lo)
    acs_t = acs.T
    last = acs[ln - 1:ln, :]
    eacs = jnp.exp2(acs)
    wend = jnp.exp2(last - acs) * dt
    lo_half = lax.broadcasted_iota(jnp.int32, (ln, LANES), 1) < HEADDIM

    def hi_mid(v):
        hi = v.astype(BF16)
        return jnp.concatenate([hi, (v - hi.astype(F32)).astype(BF16)], axis=1)

    per_head = jnp.concatenate([hi_mid(eacs), hi_mid(wend), hi_mid(dt)], axis=0)

    for g in range(groups):
        gcols = slice(g * gw, (g + 1) * gw)
        bg = xc_ref[sl_b + g]
        cg = xc_ref[sl_c + g].astype(BF16)
        cb = _dot_nt(cg, bg.astype(BF16))
        bt = bg.T.astype(BF16)
        eg = e_ref[:, gcols]
        xg = jnp.concatenate([xc_ref[g * (gw // LANES) + q] for q in range(gw // LANES)], axis=1)
        st = state_ref[:, gcols]
        lanes_g = _dot(per_head, eg)
        eacs_g = lanes_g[0:ln]
        y_off = _dot(cg, st.astype(BF16)) * eacs_g
        s_new = _dot(bt, (xg * lanes_g[ln:2 * ln]).astype(BF16))
        state_ref[:, gcols] = st * eacs_g[ln - 1:ln, :] + s_new
        xdt = xg * lanes_g[2 * ln:3 * ln]
        ys = []
        for p in range(hg // 2):
            pr = (g * hg) // 2 + p
            m_parts = []
            for hh in (2 * pr, 2 * pr + 1):
                a_row = acs_t[hh:hh + 1, :]
                a_col = jnp.broadcast_to(acs[:, hh:hh + 1], (ln, ln))
                dec = jnp.exp2(jnp.where(causal, a_col - a_row, -jnp.inf))
                m_parts.append((cb * dec).astype(BF16))
            x2 = xdt[:, p * LANES:(p + 1) * LANES]
            xa = jnp.where(lo_half, x2, 0.0).astype(BF16)
            xb = jnp.where(lo_half, 0.0, x2).astype(BF16)
            ys.append(_dot(jnp.concatenate(m_parts, axis=1), jnp.concatenate([xa, xb], axis=0)))
        yg = jnp.concatenate(ys, axis=1) + y_off + dfull_ref[:, gcols] * xg
        zg = z_ref[:, gcols].astype(F32)
        yg = yg * (zg * _sigmoid(zg))
        out_ref[:, gcols] = _rms(yg, ng_ref[:, gcols]).astype(out_ref.dtype)


def _ssd(zx, dt_raw, conv_wb, dt_bias, a_log, d_full, norm_g, cfg, t):
    ln = SSD_CHUNK
    d_inner = cfg.d_inner
    gn = cfg.ssm_groups * SSM_STATE
    cdim = d_inner + 2 * gn
    nslab = cdim // LANES
    nc = cfg.seq // ln
    m = cfg.batch * cfg.seq
    b_blk = (2 * d_inner) // gn
    row = lambda b, c: b * nc + c
    kern = functools.partial(_ssd_kernel, d_inner=d_inner, gn=gn, groups=cfg.ssm_groups)
    head_lanes = (jnp.arange(d_inner)[None, :] // HEADDIM == jnp.arange(LANES)[:, None]).astype(BF16)
    head_lanes = jnp.concatenate([head_lanes, head_lanes], axis=0)
    const = lambda b, c: (0, 0)
    return pl.pallas_call(
        kern,
        grid=(cfg.batch, nc),
        in_specs=[
            pl.BlockSpec((ln, d_inner), lambda b, c: (row(b, c), 0)),
            pl.BlockSpec((ln, d_inner), lambda b, c: (row(b, c), 1)),
            pl.BlockSpec((ln, gn), lambda b, c: (row(b, c), b_blk)),
            pl.BlockSpec((ln, gn), lambda b, c: (row(b, c), b_blk + 1)),
            pl.BlockSpec((ln, LANES), lambda b, c: (row(b, c), 0)),
            pl.BlockSpec((nslab, SUBLANES, LANES), lambda b, c: (0, 0, 0)),
            pl.BlockSpec((1, LANES), const),
            pl.BlockSpec((1, LANES), const),
            pl.BlockSpec((1, d_inner), const),
            pl.BlockSpec((1, d_inner), const),
            pl.BlockSpec((2 * LANES, d_inner), const),
        ],
        out_specs=pl.BlockSpec((ln, d_inner), lambda b, c: (row(b, c), 0)),
        out_shape=jax.ShapeDtypeStruct((m, d_inner), BF16),
        scratch_shapes=[
            pltpu.VMEM((nslab, CONV_HIST + ln, LANES), F32),
            pltpu.VMEM((nslab, ln, LANES), F32),
            pltpu.VMEM((SSM_STATE, d_inner), F32),
        ],
        compiler_params=_cparams(("parallel", "arbitrary"), t.vmem_mb),
        name="ssm_scan",
    )(zx, zx, zx, zx, dt_raw, conv_wb, dt_bias, a_log, d_full, norm_g, head_lanes)


def _matmul_res_kernel(a_ref, w_ref, r_ref, o_ref, *, tn):
    a = a_ref[...]
    for j in range(o_ref.shape[1] // tn):
        cols = slice(j * tn, (j + 1) * tn)
        o_ref[:, cols] = r_ref[:, cols] + _dot(a, w_ref[:, cols])


def _matmul_res(a, w, layer, res, t, name):
    m, k = a.shape
    n = w.shape[2]
    tm, tn = t.tm_out, min(t.tn_proj, n)
    return pl.pallas_call(
        functools.partial(_matmul_res_kernel, tn=tn),
        grid=(m // tm,),
        in_specs=[
            pl.BlockSpec((tm, k), lambda i: (i, 0)),
            pl.BlockSpec((None, k, n), lambda i: (layer, 0, 0), pipeline_mode=pl.Buffered(1)),
            pl.BlockSpec((tm, n), lambda i: (i, 0)),
        ],
        out_specs=pl.BlockSpec((tm, n), lambda i: (i, 0)),
        out_shape=jax.ShapeDtypeStruct((m, n), F32),
        compiler_params=_cparams(("parallel",), t.vmem_mb),
        name=name,
    )(a, w, res)


def _mlp_kernel(h_ref, g_ref, w1_ref, w2_ref, o_ref, hn_ref, *, rc):
    def ff(hn, w1, w2):
        a = jnp.maximum(_dot(hn, w1), 0.0)
        return _dot((a * a).astype(BF16), w2)

    @pl.when(pl.program_id(1) == 0)
    def _():
        w1 = w1_ref[...].astype(BF16)
        w2 = w2_ref[...].astype(BF16)
        for r in range(h_ref.shape[0] // rc):
            rows = slice(r * rc, (r + 1) * rc)
            h = h_ref[rows, :]
            hn = _rms(h, g_ref[...]).astype(BF16)
            hn_ref[rows, :] = hn
            o_ref[rows, :] = h + ff(hn, w1, w2)

    @pl.when(pl.program_id(1) > 0)
    def _():
        o_ref[...] += ff(hn_ref[...], w1_ref[...].astype(BF16), w2_ref[...].astype(BF16))


def _mlp(h, gain, w1, w2, layer, t):
    m, d = h.shape
    f = w1.shape[2]
    tm, tf = t.tm_mlp, t.tf_mlp
    return pl.pallas_call(
        functools.partial(_mlp_kernel, rc=min(ROW_CHUNK, tm)),
        grid=(m // tm, f // tf),
        in_specs=[
            pl.BlockSpec((tm, d), lambda i, j: (i, 0)),
            pl.BlockSpec((1, d), lambda i, j: (0, 0)),
            pl.BlockSpec((None, d, tf), lambda i, j: (layer, 0, j)),
            pl.BlockSpec((None, tf, d), lambda i, j: (layer, j, 0)),
        ],
        out_specs=pl.BlockSpec((tm, d), lambda i, j: (i, 0)),
        out_shape=jax.ShapeDtypeStruct((m, d), F32),
        scratch_shapes=[pltpu.VMEM((tm, d), BF16)],
        compiler_params=_cparams(("parallel", "arbitrary"), t.vmem_mb),
        name="sq_relu_mlp",
    )(h, gain, w1, w2)


def _ple_kernel(h_ref, g_ref, wg_ref, p_ref, wp_ref, o_ref, *, tn):
    hn = _rms(h_ref[...], g_ref[...]).astype(BF16)
    pb = p_ref[...].astype(BF16)
    for j in range(h_ref.shape[1] // tn):
        cols = slice(j * tn, (j + 1) * tn)
        gate = _sigmoid(_dot(hn, wg_ref[:, cols]))
        o_ref[:, cols] = h_ref[:, cols] + gate * _dot(pb, wp_ref[:, cols])


def _ple(h, gain, wg, p, layer, wp, t):
    m, d = h.shape
    pd = p.shape[2]
    tm = t.tm_ple
    tn = 512 if d % 512 == 0 else LANES
    return pl.pallas_call(
        functools.partial(_ple_kernel, tn=tn),
        grid=(m // tm,),
        in_specs=[
            pl.BlockSpec((tm, d), lambda i: (i, 0)),
            pl.BlockSpec((1, d), lambda i: (0, 0)),
            pl.BlockSpec((d, d), lambda i: (0, 0)),
            pl.BlockSpec((None, tm, pd), lambda i: (layer, i, 0)),
            pl.BlockSpec((pd, d), lambda i: (0, 0)),
        ],
        out_specs=pl.BlockSpec((tm, d), lambda i: (i, 0)),
        out_shape=jax.ShapeDtypeStruct((m, d), F32),
        compiler_params=_cparams(("parallel",), t.vmem_mb),
        name="per_layer_input",
    )(h, gain, wg, p, wp)


def _rope(x, cos_f, sin_s):
    lane = lax.broadcasted_iota(jnp.int32, x.shape, 1)
    half = QK_ROPE // 2
    rot = jnp.where(lane < half, pltpu.roll(x, LANES - half, 1), pltpu.roll(x, half, 1))
    return x * cos_f + rot * sin_s


def _rms_rope_part(x, g_pad):
    ms = jnp.sum(x * x, axis=-1, keepdims=True) * (1.0 / QK_ROPE)
    return x * lax.rsqrt(ms + EPS) * g_pad


def _mla_a_kernel(h_ref, gkv_ref, gq_ref, wdkv_ref, wkr_ref, wdq_ref, ckvn_ref, krn_ref, cqn_ref,
                  pos_ref, fr_ref, ckv_ref, cq_ref, kr_ref, cos_ref, sin_ref):
    h = h_ref[...]
    y = h * lax.rsqrt(jnp.mean(h * h, axis=-1, keepdims=True) + EPS)
    hkv = (y * gkv_ref[...]).astype(BF16)
    hq = (y * gq_ref[...]).astype(BF16)
    ckv_ref[...] = _rms(_dot(hkv, wdkv_ref[...]), ckvn_ref[...]).astype(BF16)
    cq_ref[...] = _rms(_dot(hq, wdq_ref[...]), cqn_ref[...]).astype(BF16)
    ang = fr_ref[...] * pos_ref[...].astype(F32)
    cos_t = jnp.cos(ang)
    sin_t = jnp.sin(ang)
    cos_ref[...] = cos_t
    sin_ref[...] = sin_t
    pad = jnp.zeros((LANES - QK_ROPE, ang.shape[1]), F32)
    cos_f = jnp.concatenate([cos_t, cos_t, pad], axis=0).T
    sin_s = jnp.concatenate([-sin_t, sin_t, pad], axis=0).T
    kr = _rms_rope_part(_dot(hkv, wkr_ref[...]), krn_ref[...])
    kr_ref[...] = _rope(kr, cos_f, sin_s).astype(BF16)


def _mla_a(h, g_kv, g_q, w_dkv, w_kr, w_dq, ckv_n, kr_n, cq_n, pos, freqs, t):
    m, d = h.shape
    kvl = w_dkv.shape[1]
    ql = w_dq.shape[1]
    tm = t.tm_mla
    rowblk = lambda w: pl.BlockSpec((tm, w), lambda i: (i, 0))
    full = lambda a: pl.BlockSpec(a.shape, lambda i: (0, 0))
    colblk = pl.BlockSpec((QK_ROPE // 2, tm), lambda i: (0, i))
    return pl.pallas_call(
        _mla_a_kernel,
        grid=(m // tm,),
        in_specs=[rowblk(d), full(g_kv), full(g_q), full(w_dkv), full(w_kr), full(w_dq), full(ckv_n),
                  full(kr_n), full(cq_n), pl.BlockSpec((1, tm), lambda i: (0, i)), full(freqs)],
        out_specs=[rowblk(kvl), rowblk(ql), rowblk(LANES), colblk, colblk],
        out_shape=[jax.ShapeDtypeStruct((m, kvl), BF16), jax.ShapeDtypeStruct((m, ql), BF16),
                   jax.ShapeDtypeStruct((m, LANES), BF16), jax.ShapeDtypeStruct((QK_ROPE // 2, m), F32),
                   jax.ShapeDtypeStruct((QK_ROPE // 2, m), F32)],
        compiler_params=_cparams(("parallel",), t.vmem_mb),
        name="mla_down_proj",
    )(h, g_kv, g_q, w_dkv, w_kr, w_dq, ckv_n, kr_n, cq_n, pos, freqs)


def _mla_b_kernel(ckv_ref, cq_ref, kr_ref, cos_ref, sin_ref, wuk_ref, wuvt_ref, wuqt_ref, knn_ref, qnn_ref,
                  qrn_ref, q_ref, k_ref, vt_ref, *, scale, heads, hblk):
    ckv = ckv_ref[...]
    cq = cq_ref[...]
    kr = kr_ref[...]
    cos_t = cos_ref[...]
    sin_t = sin_ref[...]
    half = QK_ROPE // 2
    g_nope = qnn_ref[...] * scale
    g_rope = qrn_ref[...] * scale
    tm = ckv.shape[0]
    for h0 in range(0, heads, hblk):
        kn_blk = _dot(ckv, wuk_ref[:, h0 * QK_NOPE:(h0 + hblk) * QK_NOPE])
        vt_blk = _dot_nt(wuvt_ref[h0 * V_HEAD:(h0 + hblk) * V_HEAD, :], ckv)
        qt_blk = _dot_nt(wuqt_ref[h0 * QK_PAD:(h0 + hblk) * QK_PAD, :], cq)
        for i in range(hblk):
            h = h0 + i
            kn = _rms(kn_blk[:, i * QK_NOPE:(i + 1) * QK_NOPE], knn_ref[...])
            k_ref[0, h, :, 0:QK_NOPE] = kn.astype(BF16)
            k_ref[0, h, :, QK_NOPE:QK_PAD] = kr
            vt_ref[0, h, 0, 0:V_HEAD, :] = vt_blk[i * V_HEAD:(i + 1) * V_HEAD].astype(BF16)
            vt_ref[0, h, 0, V_HEAD:V_AUG, :] = jnp.ones((V_AUG - V_HEAD, tm), BF16)
            qt = qt_blk[i * QK_PAD:(i + 1) * QK_PAD]
            qn = qt[0:QK_NOPE]
            qn = qn * lax.rsqrt(jnp.mean(qn * qn, axis=0, keepdims=True) + EPS) * g_nope
            qr = qt[QK_NOPE:QK_NOPE + QK_ROPE]
            qr = qr * lax.rsqrt(jnp.mean(qr * qr, axis=0, keepdims=True) + EPS) * g_rope
            x1 = qr[0:half]
            x2 = qr[half:QK_ROPE]
            q_ref[0, h, 0, 0:QK_NOPE, :] = qn.astype(BF16)
            q_ref[0, h, 0, QK_NOPE:QK_NOPE + half, :] = (x1 * cos_t - x2 * sin_t).astype(BF16)
            q_ref[0, h, 0, QK_NOPE + half:QK_NOPE + QK_ROPE, :] = (x2 * cos_t + x1 * sin_t).astype(BF16)
            q_ref[0, h, 0, QK_NOPE + QK_ROPE:QK_PAD, :] = jnp.zeros((QK_PAD - QK_NOPE - QK_ROPE, tm), BF16)


def _mla_b(ckv, cq, kr, cos_t, sin_t, w_uk, w_uvt, w_uqt, kn_n, qn_n, qr_n, cfg, t):
    m, kvl = ckv.shape
    ql = cq.shape[1]
    hds = cfg.mla_heads
    tm = t.t_attn
    ns = cfg.seq // tm
    scale = (QK_NOPE + QK_ROPE) ** -0.5 * LOG2E
    rowblk = lambda w: pl.BlockSpec((tm, w), lambda i: (i, 0))
    full = lambda a: pl.BlockSpec(a.shape, lambda i: (0, 0))
    colblk = pl.BlockSpec((QK_ROPE // 2, tm), lambda i: (0, i))
    return pl.pallas_call(
        functools.partial(_mla_b_kernel, scale=scale, heads=hds, hblk=t.up_heads),
        grid=(m // tm,),
        in_specs=[rowblk(kvl), rowblk(ql), rowblk(LANES), colblk, colblk,
                  full(w_uk), full(w_uvt), full(w_uqt), full(kn_n), full(qn_n), full(qr_n)],
        out_specs=[
            pl.BlockSpec((1, hds, 1, QK_PAD, tm), lambda i: (i // ns, 0, i % ns, 0, 0)),
            pl.BlockSpec((1, hds, tm, QK_PAD), lambda i: (i // ns, 0, i % ns, 0)),
            pl.BlockSpec((1, hds, 1, V_AUG, tm), lambda i: (i // ns, 0, i % ns, 0, 0)),
        ],
        out_shape=[jax.ShapeDtypeStruct((cfg.batch, hds, ns, QK_PAD, tm), BF16),
                   jax.ShapeDtypeStruct((cfg.batch, hds, cfg.seq, QK_PAD), BF16),
                   jax.ShapeDtypeStruct((cfg.batch, hds, ns, V_AUG, tm), BF16)],
        compiler_params=_cparams(("parallel",), t.vmem_mb),
        name="mla_up_proj",
    )(ckv, cq, kr, cos_t, sin_t, w_uk, w_uvt, w_uqt, kn_n, qn_n, qr_n)


def _attn_kernel(q_ref, k_ref, vt_ref, o_ref, m_ref, acc_ref, sa_ref, sb_ref, bma_ref, bmb_ref, *, tq, hb):
    qi = pl.program_id(2)
    tk = tq // 2
    m_ref[...] = jnp.full_like(m_ref, -jnp.inf)
    acc_ref[...] = jnp.zeros_like(acc_ref)

    def visible_mask(k0):
        krow = k0 + lax.broadcasted_iota(jnp.int32, (tk, tq), 0)
        qcol = qi * tq + lax.broadcasted_iota(jnp.int32, (tk, tq), 1)
        return (krow >> MASK_CHUNK_SHIFT) <= (qcol >> MASK_CHUNK_SHIFT)

    def qk_stage(j, half, s_ref, bm_ref, masked):
        k0 = j * tq + half * tk
        if masked:
            visible = visible_mask(k0)
        for hh in range(hb):
            k = k_ref[0, hh, pl.ds(pl.multiple_of(k0, tk), tk), :]
            s = _dot(k, q_ref[0, hh, 0])
            if masked:
                s = jnp.where(visible, s, -jnp.inf)
            s_ref[hh] = s
            bm_ref[hh] = jnp.max(s, axis=0, keepdims=True)

    def sm_stage(j, half, s_ref, bm_ref, masked=False):
        if masked:
            visible = visible_mask(j * tq + half * tk)
        for hh in range(hb):
            m_prev = m_ref[hh]
            if masked:
                s = jnp.where(visible, s_ref[hh], -jnp.inf)
                m_new = jnp.maximum(m_prev, jnp.max(s, axis=0, keepdims=True))
            else:
                s = s_ref[hh]
                m_new = jnp.maximum(m_prev, bm_ref[hh])
            p = jnp.exp2(s - m_new)
            alpha = jnp.exp2(m_prev - m_new)
            vt = vt_ref[0, hh, j, :, half * tk:(half + 1) * tk]
            acc_ref[hh] = acc_ref[hh] * alpha + _dot(vt, p.astype(BF16))
            m_ref[hh] = m_new

    qk_stage(0, 0, sa_ref, bma_ref, False)

    def body(j, carry):
        qk_stage(j, 1, sb_ref, bmb_ref, False)
        sm_stage(j, 0, sa_ref, bma_ref)
        qk_stage(j + 1, 0, sa_ref, bma_ref, False)
        sm_stage(j, 1, sb_ref, bmb_ref)
        return carry

    lax.fori_loop(0, qi, body, 0)
    k1 = qi * tq + tk
    hi_q = slice(tk, tq)
    krow = k1 + lax.broadcasted_iota(jnp.int32, (tk, tk), 0)
    qcol = k1 + lax.broadcasted_iota(jnp.int32, (tk, tk), 1)
    visible_q = (krow >> MASK_CHUNK_SHIFT) <= (qcol >> MASK_CHUNK_SHIFT)
    for hh in range(hb):
        k = k_ref[0, hh, pl.ds(pl.multiple_of(k1, tk), tk), :]
        s = _dot(k, q_ref[0, hh, 0, :, hi_q])
        sb_ref[hh, :, hi_q] = jnp.where(visible_q, s, -jnp.inf)
    sm_stage(qi, 0, sa_ref, bma_ref, masked=True)
    for hh in range(hb):
        s = sb_ref[hh, :, hi_q]
        m_prev = m_ref[hh, :, hi_q]
        m_new = jnp.maximum(m_prev, jnp.max(s, axis=0, keepdims=True))
        p = jnp.exp2(s - m_new)
        alpha = jnp.exp2(m_prev - m_new)
        vt = vt_ref[0, hh, qi, :, hi_q]
        acc_ref[hh, :, hi_q] = acc_ref[hh, :, hi_q] * alpha + _dot(vt, p.astype(BF16))
        m_ref[hh, :, hi_q] = m_new
    for hh in range(hb):
        o = acc_ref[hh, 0:V_HEAD, :] / acc_ref[hh, V_HEAD:V_HEAD + 1, :]
        o_ref[0, :, hh * V_HEAD:(hh + 1) * V_HEAD] = o.T.astype(o_ref.dtype)


def _attn(q, k, vt, cfg, t):
    tq = t.t_attn
    hb = t.attn_heads
    hds = cfg.mla_heads
    nq = cfg.seq // tq
    return pl.pallas_call(
        functools.partial(_attn_kernel, tq=tq, hb=hb),
        grid=(cfg.batch, hds // hb, nq),
        in_specs=[
            pl.BlockSpec((1, hb, 1, QK_PAD, tq), lambda b, h, i: (b, h, i, 0, 0)),
            pl.BlockSpec((1, hb, cfg.seq, QK_PAD), lambda b, h, i: (b, h, 0, 0)),
            pl.BlockSpec((1, hb, nq, V_AUG, tq), lambda b, h, i: (b, h, 0, 0, 0)),
        ],
        out_specs=pl.BlockSpec((1, tq, hb * V_HEAD), lambda b, h, i: (b, i, h)),
        out_shape=jax.ShapeDtypeStruct((cfg.batch, cfg.seq, hds * V_HEAD), BF16),
        scratch_shapes=[pltpu.VMEM((hb, 1, tq), F32), pltpu.VMEM((hb, V_AUG, tq), F32),
                        pltpu.VMEM((hb, tq // 2, tq), F32), pltpu.VMEM((hb, tq // 2, tq), F32),
                        pltpu.VMEM((hb, 1, tq), F32), pltpu.VMEM((hb, 1, tq), F32)],
        compiler_params=_cparams(("parallel", "parallel", "arbitrary"), t.vmem_mb),
        name="mla_flash_attn",
    )(q, k, vt)


def _pad_lanes(v, width=LANES):
    v = v.reshape(1, -1)
    return jnp.pad(v, ((0, 0), (0, width - v.shape[1])))


def _forward(cfg, t, x, p, positions, ln_mix, ln_mlp, mlp_w1, mlp_w2, ple_norm, ple_gate_w, ple_proj_w,
             ssm_in_w, ssm_conv_w, ssm_conv_b, ssm_dt_bias, ssm_a_log, ssm_d, ssm_norm, ssm_out_w,
             kv_in_norm, w_dkv, ckv_norm, w_kr, w_uk, w_uv, k_nope_norm, k_rope_norm,
             w_dq, cq_norm, w_uq, q_nope_norm, q_rope_norm, mla_out_w):
    depth = ln_mix.shape[0]
    n_a = ssm_in_w.shape[0]
    m = cfg.batch * cfg.seq
    d = cfg.d_model
    hds = cfg.mla_heads
    row = lambda v: v.reshape(1, -1).astype(F32)
    bf = lambda w: w.astype(BF16)
    h = x.reshape(m, d)
    p2 = p.reshape(depth, m, cfg.ple_dim)

    kv_ready = False
    for i in range(depth):
        if i < n_a:
            j = i
            d_inner = cfg.d_inner
            cdim = d_inner + 2 * cfg.ssm_groups * SSM_STATE
            w_in_t = jnp.swapaxes(ssm_in_w, 1, 2)
            w_dt = bf(jnp.pad(w_in_t[j, d_inner + cdim:], ((0, LANES - d_inner // HEADDIM), (0, 0))))
            zx, dt_raw = _inproj(h, row(ln_mix[i]), w_in_t, j, d_inner + cdim, w_dt, t)
            cwb = jnp.concatenate([ssm_conv_w[j], ssm_conv_b[j][None, :],
                                   jnp.zeros((SUBLANES - CONV_W - 1, cdim), F32)], axis=0)
            cwb = cwb.reshape(SUBLANES, cdim // LANES, LANES).transpose(1, 0, 2)
            yn = _ssd(zx, dt_raw, cwb, _pad_lanes(ssm_dt_bias[j]), _pad_lanes(ssm_a_log[j]),
                      row(jnp.repeat(ssm_d[j], HEADDIM)), row(ssm_norm[j]), cfg, t)
            h = _matmul_res(yn, bf(ssm_out_w), j, h, t, "ssm_out_proj")
        else:
            j = i - n_a
            if not kv_ready:
                kv_h = h
                kv_ready = True
            half = QK_ROPE // 2
            fr = ROPE_THETA ** (-jnp.arange(half, dtype=F32) / half)
            freqs = fr.reshape(half, 1)
            w_kr_p = bf(jnp.pad(w_kr, ((0, 0), (0, LANES - QK_ROPE))))
            assert depth - n_a == 1
            ckv, cq, kr, cos_t, sin_t = _mla_a(
                kv_h, row(kv_in_norm), row(ln_mix[i]), bf(w_dkv), w_kr_p, bf(w_dq[j]), row(ckv_norm),
                _pad_lanes(k_rope_norm), row(cq_norm[j]), positions.reshape(1, m), freqs, t)
            wq = w_uq[j].reshape(cfg.q_lora, hds, QK_NOPE + QK_ROPE)
            wq = jnp.pad(wq, ((0, 0), (0, 0), (0, QK_PAD - QK_NOPE - QK_ROPE))).reshape(cfg.q_lora, hds * QK_PAD)
            q, k, vt = _mla_b(ckv, cq, kr, cos_t, sin_t, bf(w_uk), bf(w_uv.T), bf(wq.T), row(k_nope_norm),
                              q_nope_norm[j].reshape(-1, 1), q_rope_norm[j].reshape(-1, 1), cfg, t)
            o = _attn(q, k, vt, cfg, t)
            h = _matmul_res(o.reshape(m, hds * V_HEAD), bf(mla_out_w), j, h, t, "mla_out_proj")
        h = _mlp(h, row(ln_mlp[i]), mlp_w1, mlp_w2, i, t)
        h = _ple(h, row(ple_norm[i]), bf(ple_gate_w[i]), p2, i, bf(ple_proj_w[i]), t)
    return h.reshape(cfg.batch, cfg.seq, d)


_CFG = _Cfg(batch=2, seq=4096, d_model=2048, d_ff=8192, ple_dim=256, d_inner=4096, ssm_groups=8,
            mla_heads=16, q_lora=512, kv_lora=512)
_TILES = _Tiles(tm_proj=1024, tm_out=512, tn_proj=512, tn_in=1024, tm_mlp=1024, tf_mlp=512, tm_ple=512, tm_mla=512, t_attn=512,
                attn_heads=4, up_heads=2, vmem_mb=60)


def kernel(x, p, positions, ln_mix, ln_mlp, mlp_w1, mlp_w2, ple_norm, ple_gate_w, ple_proj_w, ssm_in_w, ssm_conv_w, ssm_conv_b, ssm_dt_bias, ssm_a_log, ssm_d, ssm_norm, ssm_out_w, kv_in_norm, w_dkv, ckv_norm, w_kr, w_uk, w_uv, k_nope_norm, k_rope_norm, w_dq, cq_norm, w_uq, q_nope_norm, q_rope_norm, mla_out_w):
    return _forward(_CFG, _TILES, x, p, positions, ln_mix, ln_mlp, mlp_w1, mlp_w2, ple_norm, ple_gate_w,
                    ple_proj_w, ssm_in_w, ssm_conv_w, ssm_conv_b, ssm_dt_bias, ssm_a_log, ssm_d, ssm_norm,
                    ssm_out_w, kv_in_norm, w_dkv, ckv_norm, w_kr, w_uk, w_uv, k_nope_norm, k_rope_norm,
                    w_dq, cq_norm, w_uq, q_nope_norm, q_rope_norm, mla_out_w)
```

```python
import functools
import math
from typing import NamedTuple

import jax
import jax.numpy as jnp
from jax import lax
from jax.experimental import pallas as pl
from jax.experimental.pallas import tpu as pltpu

F32 = jnp.float32
BF16 = jnp.bfloat16
EPS = 1e-6

LANES = 128
SUBLANES = 8
SSD_CHUNK = 128
HEADDIM = 64
SSM_STATE = 128
CONV_W = 4
CONV_HIST = 8
CONV_STRIDE = 4
ROW_CHUNK = 256
MASK_CHUNK_SHIFT = 6
QK_NOPE = 128
QK_ROPE = 64
V_HEAD = 128
QK_PAD = 256
V_AUG = 144
ROPE_THETA = 10000.0
LOG2E = math.log2(math.e)
MIB = 1024 * 1024


class _Cfg(NamedTuple):
    batch: int
    seq: int
    d_model: int
    d_ff: int
    ple_dim: int
    d_inner: int
    ssm_groups: int
    mla_heads: int
    q_lora: int
    kv_lora: int


class _Tiles(NamedTuple):
    tm_proj: int
    tm_out: int
    tn_proj: int
    tn_in: int
    tm_mlp: int
    tf_mlp: int
    tm_ple: int
    tm_mla: int
    t_attn: int
    attn_heads: int
    up_heads: int
    vmem_mb: int


def _cparams(sem, vmem_mb):
    return pltpu.CompilerParams(dimension_semantics=sem, vmem_limit_bytes=vmem_mb * MIB)


def _rms(x, g):
    return x * lax.rsqrt(jnp.mean(x * x, axis=-1, keepdims=True) + EPS) * g


def _sigmoid(x):
    return 1.0 / (1.0 + jnp.exp2(x * (-LOG2E)))


def _softplus(x):
    return jnp.maximum(x, 0.0) + jnp.log1p(jnp.exp(-jnp.abs(x)))


def _dot(a, b):
    return jnp.dot(a, b, preferred_element_type=F32)


def _dot_nt(a, b):
    return lax.dot_general(a, b, (((1,), (1,)), ((), ())), preferred_element_type=F32)


def _split3(v):
    hi = v.astype(BF16)
    r = v - hi.astype(F32)
    mid = r.astype(BF16)
    lo = (r - mid.astype(F32)).astype(BF16)
    return hi, mid, lo


def _inproj_kernel(h_ref, g_ref, w_ref, wdt_ref, zx_ref, dt_ref, hn_ref, *, rc):
    @pl.when(pl.program_id(1) == 0)
    def _():
        w = w_ref[...].astype(BF16)
        for r in range(h_ref.shape[0] // rc):
            rows = slice(r * rc, (r + 1) * rc)
            hn = _rms(h_ref[rows, :], g_ref[...]).astype(BF16)
            hn_ref[rows, :] = hn
            dt_ref[rows, :] = _dot_nt(hn, wdt_ref[...])
            zx_ref[rows, :] = _dot_nt(hn, w).astype(zx_ref.dtype)

    @pl.when(pl.program_id(1) > 0)
    def _():
        zx_ref[...] = _dot_nt(hn_ref[...], w_ref[...].astype(BF16)).astype(zx_ref.dtype)


def _inproj(h, gain, w_in_t, layer, n, w_dt, t):
    m, k = h.shape
    tm, tn = t.tm_proj, t.tn_in
    return pl.pallas_call(
        functools.partial(_inproj_kernel, rc=min(ROW_CHUNK, tm)),
        grid=(m // tm, n // tn),
        in_specs=[
            pl.BlockSpec((tm, k), lambda i, j: (i, 0)),
            pl.BlockSpec((1, k), lambda i, j: (0, 0)),
            pl.BlockSpec((None, tn, k), lambda i, j: (layer, j, 0)),
            pl.BlockSpec((LANES, k), lambda i, j: (0, 0)),
        ],
        out_specs=[
            pl.BlockSpec((tm, tn), lambda i, j: (i, j)),
            pl.BlockSpec((tm, LANES), lambda i, j: (i, 0)),
        ],
        out_shape=[jax.ShapeDtypeStruct((m, n), BF16), jax.ShapeDtypeStruct((m, LANES), F32)],
        scratch_shapes=[pltpu.VMEM((tm, k), BF16)],
        compiler_params=_cparams(("parallel", "arbitrary"), t.vmem_mb),
        name="ssm_in_proj",
    )(h, gain, w_in_t, w_dt)


def _ssd_kernel(z_ref, x_ref, b_ref, c_ref, dt_ref, cwb_ref, dtb_ref, alog_ref, dfull_ref, ng_ref, e_ref,
                out_ref, buf_ref, xc_ref, state_ref, *, d_inner, gn, groups):
    ln = SSD_CHUNK
    cdim = d_inner + 2 * gn
    nslab = cdim // LANES
    sl_b = d_inner // LANES
    sl_c = (d_inner + gn) // LANES
    heads = d_inner // HEADDIM
    hg = heads // groups
    gw = hg * HEADDIM
    c = pl.program_id(1)

    @pl.when(c == 0)
    def _():
        buf_ref[:, 0:CONV_HIST, :] = jnp.zeros((nslab, CONV_HIST, LANES), F32)
        state_ref[...] = jnp.zeros_like(state_ref)

    @pl.when(c > 0)
    def _():
        buf_ref[:, 0:CONV_HIST, :] = buf_ref[:, ln:ln + CONV_HIST, :]

    for s in range(nslab):
        lo = s * LANES
        if lo < d_inner:
            src = x_ref[:, lo:lo + LANES]
        elif lo < d_inner + gn:
            src = b_ref[:, lo - d_inner:lo - d_inner + LANES]
        else:
            src = c_ref[:, lo - d_inner - gn:lo - d_inner - gn + LANES]
        buf_ref[s, CONV_HIST:CONV_HIST + ln, :] = src.astype(F32)

    rows_per_blk = SUBLANES * CONV_STRIDE

    def conv_slab(s, carry):
        wb = cwb_ref[s]
        for blk in range(ln // rows_per_blk):
            for i in range(CONV_STRIDE):
                t0 = blk * rows_per_blk + i
                acc = wb[CONV_W:CONV_W + 1, :]
                for k in range(CONV_W):
                    start = CONV_HIST + t0 - (CONV_W - 1) + k
                    acc = acc + wb[k:k + 1, :] * buf_ref[s, pl.ds(start, SUBLANES, stride=CONV_STRIDE), :]
                xc_ref[s, pl.ds(t0, SUBLANES, stride=CONV_STRIDE), :] = acc + acc * jnp.tanh(acc)
        return carry

    lax.fori_loop(0, nslab, conv_slab, 0, unroll=8)

    dt = _softplus(dt_ref[...] + dtb_ref[...])
    a = dt * (-jnp.exp(alog_ref[...])) * LOG2E
    row = lax.broadcasted_iota(jnp.int32, (ln, ln), 0)
    col = lax.broadcasted_iota(jnp.int32, (ln, ln), 1)
    causal = row >= col
    tril = jnp.where(causal, 1.0, 0.0).astype(BF16)
    a_hi, a_mid, a_lo = _split3(a)
    acs = _dot(tril, a_hi) + _dot(tril, a_mid) + _dot(tril, a_lo)
    acs_t = acs.T
    last = acs[ln - 1:ln, :]
    eacs = jnp.exp2(acs)
    wend = jnp.exp2(last - acs) * dt
    lo_half = lax.broadcasted_iota(jnp.int32, (ln, LANES), 1) < HEADDIM

    def hi_mid(v):
        hi = v.astype(BF16)
        return jnp.concatenate([hi, (v - hi.astype(F32)).astype(BF16)], axis=1)

    per_head = jnp.concatenate([hi_mid(eacs), hi_mid(wend), hi_mid(dt)], axis=0)

    for g in range(groups):
        gcols = slice(g * gw, (g + 1) * gw)
        bg = xc_ref[sl_b + g]
        cg = xc_ref[sl_c + g].astype(BF16)
        cb = _dot_nt(cg, bg.astype(BF16))
        bt = bg.T.astype(BF16)
        eg = e_ref[:, gcols]
        xg = jnp.concatenate([xc_ref[g * (gw // LANES) + q] for q in range(gw // LANES)], axis=1)
        st = state_ref[:, gcols]
        lanes_g = _dot(per_head, eg)
        eacs_g = lanes_g[0:ln]
        y_off = _dot(cg, st.astype(BF16)) * eacs_g
        s_new = _dot(bt, (xg * lanes_g[ln:2 * ln]).astype(BF16))
        state_ref[:, gcols] = st * eacs_g[ln - 1:ln, :] + s_new
        xdt = xg * lanes_g[2 * ln:3 * ln]
        ys = []
        for p in range(hg // 2):
            pr = (g * hg) // 2 + p
            m_parts = []
            for hh in (2 * pr, 2 * pr + 1):
                a_row = acs_t[hh:hh + 1, :]
                a_col = jnp.broadcast_to(acs[:, hh:hh + 1], (ln, ln))
                dec = jnp.exp2(jnp.where(causal, a_col - a_row, -jnp.inf))
                m_parts.append((cb * dec).astype(BF16))
            x2 = xdt[:, p * LANES:(p + 1) * LANES]
            xa = jnp.where(lo_half, x2, 0.0).astype(BF16)
            xb = jnp.where(lo_half, 0.0, x2).astype(BF16)
            ys.append(_dot(jnp.concatenate(m_parts, axis=1), jnp.concatenate([xa, xb], axis=0)))
        yg = jnp.concatenate(ys, axis=1) + y_off + dfull_ref[:, gcols] * xg
        zg = z_ref[:, gcols].astype(F32)
        hz = 0.5 * zg
        yg = yg * (hz + hz * jnp.tanh(hz))
        out_ref[:, gcols] = _rms(yg, ng_ref[:, gcols]).astype(out_ref.dtype)


def _ssd(zx, dt_raw, conv_wb, dt_bias, a_log, d_full, norm_g, cfg, t):
    ln = SSD_CHUNK
    d_inner = cfg.d_inner
    gn = cfg.ssm_groups * SSM_STATE
    cdim = d_inner + 2 * gn
    nslab = cdim // LANES
    nc = cfg.seq // ln
    m = cfg.batch * cfg.seq
    b_blk = (2 * d_inner) // gn
    row = lambda b, c: b * nc + c
    kern = functools.partial(_ssd_kernel, d_inner=d_inner, gn=gn, groups=cfg.ssm_groups)
    head_lanes = (jnp.arange(d_inner)[None, :] // HEADDIM == jnp.arange(LANES)[:, None]).astype(BF16)
    head_lanes = jnp.concatenate([head_lanes, head_lanes], axis=0)
    const = lambda b, c: (0, 0)
    return pl.pallas_call(
        kern,
        grid=(cfg.batch, nc),
        in_specs=[
            pl.BlockSpec((ln, d_inner), lambda b, c: (row(b, c), 0)),
            pl.BlockSpec((ln, d_inner), lambda b, c: (row(b, c), 1)),
            pl.BlockSpec((ln, gn), lambda b, c: (row(b, c), b_blk)),
            pl.BlockSpec((ln, gn), lambda b, c: (row(b, c), b_blk + 1)),
            pl.BlockSpec((ln, LANES), lambda b, c: (row(b, c), 0)),
            pl.BlockSpec((nslab, SUBLANES, LANES), lambda b, c: (0, 0, 0)),
            pl.BlockSpec((1, LANES), const),
            pl.BlockSpec((1, LANES), const),
            pl.BlockSpec((1, d_inner), const),
            pl.BlockSpec((1, d_inner), const),
            pl.BlockSpec((2 * LANES, d_inner), const),
        ],
        out_specs=pl.BlockSpec((ln, d_inner), lambda b, c: (row(b, c), 0)),
        out_shape=jax.ShapeDtypeStruct((m, d_inner), BF16),
        scratch_shapes=[
            pltpu.VMEM((nslab, CONV_HIST + ln, LANES), F32),
            pltpu.VMEM((nslab, ln, LANES), F32),
            pltpu.VMEM((SSM_STATE, d_inner), F32),
        ],
        compiler_params=_cparams(("parallel", "arbitrary"), t.vmem_mb),
        name="ssm_scan",
    )(zx, zx, zx, zx, dt_raw, conv_wb, dt_bias, a_log, d_full, norm_g, head_lanes)


def _matmul_res_kernel(a_ref, w_ref, r_ref, o_ref, *, tn):
    a = a_ref[...]
    for j in range(o_ref.shape[1] // tn):
        cols = slice(j * tn, (j + 1) * tn)
        o_ref[:, cols] = r_ref[:, cols] + _dot(a, w_ref[:, cols])


def _matmul_res(a, w, layer, res, t, name):
    m, k = a.shape
    n = w.shape[2]
    tm, tn = t.tm_out, min(t.tn_proj, n)
    return pl.pallas_call(
        functools.partial(_matmul_res_kernel, tn=tn),
        grid=(m // tm,),
        in_specs=[
            pl.BlockSpec((tm, k), lambda i: (i, 0)),
            pl.BlockSpec((None, k, n), lambda i: (layer, 0, 0), pipeline_mode=pl.Buffered(1)),
            pl.BlockSpec((tm, n), lambda i: (i, 0)),
        ],
        out_specs=pl.BlockSpec((tm, n), lambda i: (i, 0)),
        out_shape=jax.ShapeDtypeStruct((m, n), F32),
        compiler_params=_cparams(("parallel",), t.vmem_mb),
        name=name,
    )(a, w, res)


def _mlp_kernel(h_ref, g_ref, w1_ref, w2_ref, o_ref, hn_ref, *, rc):
    def ff(hn, w1, w2):
        a = jnp.maximum(_dot(hn, w1), 0.0)
        return _dot((a * a).astype(BF16), w2)

    @pl.when(pl.program_id(1) == 0)
    def _():
        w1 = w1_ref[...].astype(BF16)
        w2 = w2_ref[...].astype(BF16)
        for r in range(h_ref.shape[0] // rc):
            rows = slice(r * rc, (r + 1) * rc)
            h = h_ref[rows, :]
            hn = _rms(h, g_ref[...]).astype(BF16)
            hn_ref[rows, :] = hn
            o_ref[rows, :] = h + ff(hn, w1, w2)

    @pl.when(pl.program_id(1) > 0)
    def _():
        o_ref[...] += ff(hn_ref[...], w1_ref[...].astype(BF16), w2_ref[...].astype(BF16))


def _mlp(h, gain, w1, w2, layer, t):
    m, d = h.shape
    f = w1.shape[2]
    tm, tf = t.tm_mlp, t.tf_mlp
    return pl.pallas_call(
        functools.partial(_mlp_kernel, rc=min(ROW_CHUNK, tm)),
        grid=(m // tm, f // tf),
        in_specs=[
            pl.BlockSpec((tm, d), lambda i, j: (i, 0)),
            pl.BlockSpec((1, d), lambda i, j: (0, 0)),
            pl.BlockSpec((None, d, tf), lambda i, j: (layer, 0, j)),
            pl.BlockSpec((None, tf, d), lambda i, j: (layer, j, 0)),
        ],
        out_specs=pl.BlockSpec((tm, d), lambda i, j: (i, 0)),
        out_shape=jax.ShapeDtypeStruct((m, d), F32),
        scratch_shapes=[pltpu.VMEM((tm, d), BF16)],
        compiler_params=_cparams(("parallel", "arbitrary"), t.vmem_mb),
        name="sq_relu_mlp",
    )(h, gain, w1, w2)


def _ple_kernel(h_ref, g_ref, wg_ref, p_ref, wp_ref, o_ref, *, tn):
    hn = _rms(h_ref[...], g_ref[...]).astype(BF16)
    pb = p_ref[...].astype(BF16)
    for j in range(h_ref.shape[1] // tn):
        cols = slice(j * tn, (j + 1) * tn)
        gate = _sigmoid(_dot(hn, wg_ref[:, cols]))
        o_ref[:, cols] = h_ref[:, cols] + gate * _dot(pb, wp_ref[:, cols])


def _ple(h, gain, wg, p, layer, wp, t):
    m, d = h.shape
    pd = p.shape[2]
    tm = t.tm_ple
    tn = 512 if d % 512 == 0 else LANES
    return pl.pallas_call(
        functools.partial(_ple_kernel, tn=tn),
        grid=(m // tm,),
        in_specs=[
            pl.BlockSpec((tm, d), lambda i: (i, 0)),
            pl.BlockSpec((1, d), lambda i: (0, 0)),
            pl.BlockSpec((d, d), lambda i: (0, 0)),
            pl.BlockSpec((None, tm, pd), lambda i: (layer, i, 0)),
            pl.BlockSpec((pd, d), lambda i: (0, 0)),
        ],
        out_specs=pl.BlockSpec((tm, d), lambda i: (i, 0)),
        out_shape=jax.ShapeDtypeStruct((m, d), F32),
        compiler_params=_cparams(("parallel",), t.vmem_mb),
        name="per_layer_input",
    )(h, gain, wg, p, wp)


def _rope(x, cos_f, sin_s):
    lane = lax.broadcasted_iota(jnp.int32, x.shape, 1)
    half = QK_ROPE // 2
    rot = jnp.where(lane < half, pltpu.roll(x, LANES - half, 1), pltpu.roll(x, half, 1))
    return x * cos_f + rot * sin_s


def _rms_rope_part(x, g_pad):
    ms = jnp.sum(x * x, axis=-1, keepdims=True) * (1.0 / QK_ROPE)
    return x * lax.rsqrt(ms + EPS) * g_pad


def _mla_a_kernel(h_ref, gkv_ref, gq_ref, wdkv_ref, wkr_ref, wdq_ref, ckvn_ref, krn_ref, cqn_ref,
                  pos_ref, fr_ref, ckv_ref, cq_ref, kr_ref, cos_ref, sin_ref):
    h = h_ref[...]
    y = h * lax.rsqrt(jnp.mean(h * h, axis=-1, keepdims=True) + EPS)
    hkv = (y * gkv_ref[...]).astype(BF16)
    hq = (y * gq_ref[...]).astype(BF16)
    ckv_ref[...] = _rms(_dot(hkv, wdkv_ref[...]), ckvn_ref[...]).astype(BF16)
    cq_ref[...] = _rms(_dot(hq, wdq_ref[...]), cqn_ref[...]).astype(BF16)
    ang = fr_ref[...] * pos_ref[...].astype(F32)
    cos_t = jnp.cos(ang)
    sin_t = jnp.sin(ang)
    cos_ref[...] = cos_t
    sin_ref[...] = sin_t
    pad = jnp.zeros((LANES - QK_ROPE, ang.shape[1]), F32)
    cos_f = jnp.concatenate([cos_t, cos_t, pad], axis=0).T
    sin_s = jnp.concatenate([-sin_t, sin_t, pad], axis=0).T
    kr = _rms_rope_part(_dot(hkv, wkr_ref[...]), krn_ref[...])
    kr_ref[...] = _rope(kr, cos_f, sin_s).astype(BF16)


def _mla_a(h, g_kv, g_q, w_dkv, w_kr, w_dq, ckv_n, kr_n, cq_n, pos, freqs, t):
    m, d = h.shape
    kvl = w_dkv.shape[1]
    ql = w_dq.shape[1]
    tm = t.tm_mla
    rowblk = lambda w: pl.BlockSpec((tm, w), lambda i: (i, 0))
    full = lambda a: pl.BlockSpec(a.shape, lambda i: (0, 0))
    colblk = pl.BlockSpec((QK_ROPE // 2, tm), lambda i: (0, i))
    return pl.pallas_call(
        _mla_a_kernel,
        grid=(m // tm,),
        in_specs=[rowblk(d), full(g_kv), full(g_q), full(w_dkv), full(w_kr), full(w_dq), full(ckv_n),
                  full(kr_n), full(cq_n), pl.BlockSpec((1, tm), lambda i: (0, i)), full(freqs)],
        out_specs=[rowblk(kvl), rowblk(ql), rowblk(LANES), colblk, colblk],
        out_shape=[jax.ShapeDtypeStruct((m, kvl), BF16), jax.ShapeDtypeStruct((m, ql), BF16),
                   jax.ShapeDtypeStruct((m, LANES), BF16), jax.ShapeDtypeStruct((QK_ROPE // 2, m), F32),
                   jax.ShapeDtypeStruct((QK_ROPE // 2, m), F32)],
        compiler_params=_cparams(("parallel",), t.vmem_mb),
        name="mla_down_proj",
    )(h, g_kv, g_q, w_dkv, w_kr, w_dq, ckv_n, kr_n, cq_n, pos, freqs)


def _mla_b_kernel(ckv_ref, cq_ref, kr_ref, cos_ref, sin_ref, wuk_ref, wuvt_ref, wuqt_ref, knn_ref, qnn_ref,
                  qrn_ref, q_ref, k_ref, vt_ref, *, scale, heads, hblk):
    ckv = ckv_ref[...]
    cq = cq_ref[...]
    kr = kr_ref[...]
    cos_t = cos_ref[...]
    sin_t = sin_ref[...]
    half = QK_ROPE // 2
    g_nope = qnn_ref[...] * scale
    g_rope = qrn_ref[...] * scale
    tm = ckv.shape[0]
    for h0 in range(0, heads, hblk):
        kn_blk = _dot(ckv, wuk_ref[:, h0 * QK_NOPE:(h0 + hblk) * QK_NOPE])
        vt_blk = _dot_nt(wuvt_ref[h0 * V_HEAD:(h0 + hblk) * V_HEAD, :], ckv)
        qt_blk = _dot_nt(wuqt_ref[h0 * QK_PAD:(h0 + hblk) * QK_PAD, :], cq)
        for i in range(hblk):
            h = h0 + i
            kn = _rms(kn_blk[:, i * QK_NOPE:(i + 1) * QK_NOPE], knn_ref[...])
            k_ref[0, h, :, 0:QK_NOPE] = kn.astype(BF16)
            k_ref[0, h, :, QK_NOPE:QK_PAD] = kr
            vt_ref[0, h, 0, 0:V_HEAD, :] = vt_blk[i * V_HEAD:(i + 1) * V_HEAD].astype(BF16)
            vt_ref[0, h, 0, V_HEAD:V_AUG, :] = jnp.ones((V_AUG - V_HEAD, tm), BF16)
            qt = qt_blk[i * QK_PAD:(i + 1) * QK_PAD]
            qn = qt[0:QK_NOPE]
            qn = qn * lax.rsqrt(jnp.mean(qn * qn, axis=0, keepdims=True) + EPS) * g_nope
            qr = qt[QK_NOPE:QK_NOPE + QK_ROPE]
            qr = qr * lax.rsqrt(jnp.mean(qr * qr, axis=0, keepdims=True) + EPS) * g_rope
            x1 = qr[0:half]
            x2 = qr[half:QK_ROPE]
            q_ref[0, h, 0, 0:QK_NOPE, :] = qn.astype(BF16)
            q_ref[0, h, 0, QK_NOPE:QK_NOPE + half, :] = (x1 * cos_t - x2 * sin_t).astype(BF16)
            q_ref[0, h, 0, QK_NOPE + half:QK_NOPE + QK_ROPE, :] = (x2 * cos_t + x1 * sin_t).astype(BF16)
            q_ref[0, h, 0, QK_NOPE + QK_ROPE:QK_PAD, :] = jnp.zeros((QK_PAD - QK_NOPE - QK_ROPE, tm), BF16)


def _mla_b(ckv, cq, kr, cos_t, sin_t, w_uk, w_uvt, w_uqt, kn_n, qn_n, qr_n, cfg, t):
    m, kvl = ckv.shape
    ql = cq.shape[1]
    hds = cfg.mla_heads
    tm = t.t_attn
    ns = cfg.seq // tm
    scale = (QK_NOPE + QK_ROPE) ** -0.5 * LOG2E
    rowblk = lambda w: pl.BlockSpec((tm, w), lambda i: (i, 0))
    full = lambda a: pl.BlockSpec(a.shape, lambda i: (0, 0))
    colblk = pl.BlockSpec((QK_ROPE // 2, tm), lambda i: (0, i))
    return pl.pallas_call(
        functools.partial(_mla_b_kernel, scale=scale, heads=hds, hblk=t.up_heads),
        grid=(m // tm,),
        in_specs=[rowblk(kvl), rowblk(ql), rowblk(LANES), colblk, colblk,
                  full(w_uk), full(w_uvt), full(w_uqt), full(kn_n), full(qn_n), full(qr_n)],
        out_specs=[
            pl.BlockSpec((1, hds, 1, QK_PAD, tm), lambda i: (i // ns, 0, i % ns, 0, 0)),
            pl.BlockSpec((1, hds, tm, QK_PAD), lambda i: (i // ns, 0, i % ns, 0)),
            pl.BlockSpec((1, hds, 1, V_AUG, tm), lambda i: (i // ns, 0, i % ns, 0, 0)),
        ],
        out_shape=[jax.ShapeDtypeStruct((cfg.batch, hds, ns, QK_PAD, tm), BF16),
                   jax.ShapeDtypeStruct((cfg.batch, hds, cfg.seq, QK_PAD), BF16),
                   jax.ShapeDtypeStruct((cfg.batch, hds, ns, V_AUG, tm), BF16)],
        compiler_params=_cparams(("parallel",), t.vmem_mb),
        name="mla_up_proj",
    )(ckv, cq, kr, cos_t, sin_t, w_uk, w_uvt, w_uqt, kn_n, qn_n, qr_n)


def _attn_kernel(q_ref, k_ref, vt_ref, o_ref, m_ref, acc_ref, sa_ref, sb_ref, bma_ref, bmb_ref, *, tq, hb):
    qi = pl.program_id(2)
    tk = tq // 2
    m_ref[...] = jnp.full_like(m_ref, -jnp.inf)
    acc_ref[...] = jnp.zeros_like(acc_ref)

    def visible_mask(k0):
        krow = k0 + lax.broadcasted_iota(jnp.int32, (tk, tq), 0)
        qcol = qi * tq + lax.broadcasted_iota(jnp.int32, (tk, tq), 1)
        return (krow >> MASK_CHUNK_SHIFT) <= (qcol >> MASK_CHUNK_SHIFT)

    def qk_stage(j, half, s_ref, bm_ref, masked):
        k0 = j * tq + half * tk
        if masked:
            visible = visible_mask(k0)
        for hh in range(hb):
            k = k_ref[0, hh, pl.ds(pl.multiple_of(k0, tk), tk), :]
            s = _dot(k, q_ref[0, hh, 0])
            if masked:
                s = jnp.where(visible, s, -jnp.inf)
            s_ref[hh] = s
            bm_ref[hh] = jnp.max(s, axis=0, keepdims=True)

    def sm_stage(j, half, s_ref, bm_ref, masked=False):
        if masked:
            visible = visible_mask(j * tq + half * tk)
        for hh in range(hb):
            m_prev = m_ref[hh]
            if masked:
                s = jnp.where(visible, s_ref[hh], -jnp.inf)
                m_new = jnp.maximum(m_prev, jnp.max(s, axis=0, keepdims=True))
            else:
                s = s_ref[hh]
                m_new = jnp.maximum(m_prev, bm_ref[hh])
            p = jnp.exp2(s - m_new)
            alpha = jnp.exp2(m_prev - m_new)
            vt = vt_ref[0, hh, j, :, half * tk:(half + 1) * tk]
            acc_ref[hh] = acc_ref[hh] * alpha + _dot(vt, p.astype(BF16))
            m_ref[hh] = m_new

    qk_stage(0, 0, sa_ref, bma_ref, False)

    def body(j, carry):
        qk_stage(j, 1, sb_ref, bmb_ref, False)
        sm_stage(j, 0, sa_ref, bma_ref)
        qk_stage(j + 1, 0, sa_ref, bma_ref, False)
        sm_stage(j, 1, sb_ref, bmb_ref)
        return carry

    lax.fori_loop(0, qi, body, 0)
    k1 = qi * tq + tk
    hi_q = slice(tk, tq)
    krow = k1 + lax.broadcasted_iota(jnp.int32, (tk, tk), 0)
    qcol = k1 + lax.broadcasted_iota(jnp.int32, (tk, tk), 1)
    visible_q = (krow >> MASK_CHUNK_SHIFT) <= (qcol >> MASK_CHUNK_SHIFT)
    for hh in range(hb):
        k = k_ref[0, hh, pl.ds(pl.multiple_of(k1, tk), tk), :]
        s = _dot(k, q_ref[0, hh, 0, :, hi_q])
        sb_ref[hh, :, hi_q] = jnp.where(visible_q, s, -jnp.inf)
    sm_stage(qi, 0, sa_ref, bma_ref, masked=True)
    for hh in range(hb):
        s = sb_ref[hh, :, hi_q]
        m_prev = m_ref[hh, :, hi_q]
        m_new = jnp.maximum(m_prev, jnp.max(s, axis=0, keepdims=True))
        p = jnp.exp2(s - m_new)
        alpha = jnp.exp2(m_prev - m_new)
        vt = vt_ref[0, hh, qi, :, hi_q]
        acc_ref[hh, :, hi_q] = acc_ref[hh, :, hi_q] * alpha + _dot(vt, p.astype(BF16))
        m_ref[hh, :, hi_q] = m_new
    for hh in range(hb):
        o = acc_ref[hh, 0:V_HEAD, :] / acc_ref[hh, V_HEAD:V_HEAD + 1, :]
        o_ref[0, :, hh * V_HEAD:(hh + 1) * V_HEAD] = o.T.astype(o_ref.dtype)


def _attn(q, k, vt, cfg, t):
    tq = t.t_attn
    hb = t.attn_heads
    hds = cfg.mla_heads
    nq = cfg.seq // tq
    return pl.pallas_call(
        functools.partial(_attn_kernel, tq=tq, hb=hb),
        grid=(cfg.batch, hds // hb, nq),
        in_specs=[
            pl.BlockSpec((1, hb, 1, QK_PAD, tq), lambda b, h, i: (b, h, i, 0, 0)),
            pl.BlockSpec((1, hb, cfg.seq, QK_PAD), lambda b, h, i: (b, h, 0, 0)),
            pl.BlockSpec((1, hb, nq, V_AUG, tq), lambda b, h, i: (b, h, 0, 0, 0)),
        ],
        out_specs=pl.BlockSpec((1, tq, hb * V_HEAD), lambda b, h, i: (b, i, h)),
        out_shape=jax.ShapeDtypeStruct((cfg.batch, cfg.seq, hds * V_HEAD), BF16),
        scratch_shapes=[pltpu.VMEM((hb, 1, tq), F32), pltpu.VMEM((hb, V_AUG, tq), F32),
                        pltpu.VMEM((hb, tq // 2, tq), F32), pltpu.VMEM((hb, tq // 2, tq), F32),
                        pltpu.VMEM((hb, 1, tq), F32), pltpu.VMEM((hb, 1, tq), F32)],
        compiler_params=_cparams(("parallel", "parallel", "arbitrary"), t.vmem_mb),
        name="mla_flash_attn",
    )(q, k, vt)


def _pad_lanes(v, width=LANES):
    v = v.reshape(1, -1)
    return jnp.pad(v, ((0, 0), (0, width - v.shape[1])))


def _forward(cfg, t, x, p, positions, ln_mix, ln_mlp, mlp_w1, mlp_w2, ple_norm, ple_gate_w, ple_proj_w,
             ssm_in_w, ssm_conv_w, ssm_conv_b, ssm_dt_bias, ssm_a_log, ssm_d, ssm_norm, ssm_out_w,
             kv_in_norm, w_dkv, ckv_norm, w_kr, w_uk, w_uv, k_nope_norm, k_rope_norm,
             w_dq, cq_norm, w_uq, q_nope_norm, q_rope_norm, mla_out_w):
    depth = ln_mix.shape[0]
    n_a = ssm_in_w.shape[0]
    m = cfg.batch * cfg.seq
    d = cfg.d_model
    hds = cfg.mla_heads
    row = lambda v: v.reshape(1, -1).astype(F32)
    bf = lambda w: w.astype(BF16)
    h = x.reshape(m, d)
    p2 = p.reshape(depth, m, cfg.ple_dim)

    kv_ready = False
    for i in range(depth):
        if i < n_a:
            j = i
            d_inner = cfg.d_inner
            cdim = d_inner + 2 * cfg.ssm_groups * SSM_STATE
            w_in_t = jnp.swapaxes(ssm_in_w, 1, 2)
            w_dt = bf(jnp.pad(w_in_t[j, d_inner + cdim:], ((0, LANES - d_inner // HEADDIM), (0, 0))))
            zx, dt_raw = _inproj(h, row(ln_mix[i]), w_in_t, j, d_inner + cdim, w_dt, t)
            cwb = jnp.concatenate([0.5 * ssm_conv_w[j], 0.5 * ssm_conv_b[j][None, :],
                                   jnp.zeros((SUBLANES - CONV_W - 1, cdim), F32)], axis=0)
            cwb = cwb.reshape(SUBLANES, cdim // LANES, LANES).transpose(1, 0, 2)
            yn = _ssd(zx, dt_raw, cwb, _pad_lanes(ssm_dt_bias[j]), _pad_lanes(ssm_a_log[j]),
                      row(jnp.repeat(ssm_d[j], HEADDIM)), row(ssm_norm[j]), cfg, t)
            h = _matmul_res(yn, bf(ssm_out_w), j, h, t, "ssm_out_proj")
        else:
            j = i - n_a
            if not kv_ready:
                kv_h = h
                kv_ready = True
            half = QK_ROPE // 2
            fr = ROPE_THETA ** (-jnp.arange(half, dtype=F32) / half)
            freqs = fr.reshape(half, 1)
            w_kr_p = bf(jnp.pad(w_kr, ((0, 0), (0, LANES - QK_ROPE))))
            assert depth - n_a == 1
            ckv, cq, kr, cos_t, sin_t = _mla_a(
                kv_h, row(kv_in_norm), row(ln_mix[i]), bf(w_dkv), w_kr_p, bf(w_dq[j]), row(ckv_norm),
                _pad_lanes(k_rope_norm), row(cq_norm[j]), positions.reshape(1, m), freqs, t)
            wq = w_uq[j].reshape(cfg.q_lora, hds, QK_NOPE + QK_ROPE)
            wq = jnp.pad(wq, ((0, 0), (0, 0), (0, QK_PAD - QK_NOPE - QK_ROPE))).reshape(cfg.q_lora, hds * QK_PAD)
            q, k, vt = _mla_b(ckv, cq, kr, cos_t, sin_t, bf(w_uk), bf(w_uv.T), bf(wq.T), row(k_nope_norm),
                              q_nope_norm[j].reshape(-1, 1), q_rope_norm[j].reshape(-1, 1), cfg, t)
            o = _attn(q, k, vt, cfg, t)
            h = _matmul_res(o.reshape(m, hds * V_HEAD), bf(mla_out_w), j, h, t, "mla_out_proj")
        h = _mlp(h, row(ln_mlp[i]), mlp_w1, mlp_w2, i, t)
        h = _ple(h, row(ple_norm[i]), bf(ple_gate_w[i]), p2, i, bf(ple_proj_w[i]), t)
    return h.reshape(cfg.batch, cfg.seq, d)


_CFG = _Cfg(batch=2, seq=4096, d_model=2048, d_ff=8192, ple_dim=256, d_inner=4096, ssm_groups=8,
            mla_heads=16, q_lora=512, kv_lora=512)
_TILES = _Tiles(tm_proj=1024, tm_out=512, tn_proj=512, tn_in=1024, tm_mlp=1024, tf_mlp=512, tm_ple=512, tm_mla=512, t_attn=512,
                attn_heads=4, up_heads=2, vmem_mb=60)


def kernel(x, p, positions, ln_mix, ln_mlp, mlp_w1, mlp_w2, ple_norm, ple_gate_w, ple_proj_w, ssm_in_w, ssm_conv_w, ssm_conv_b, ssm_dt_bias, ssm_a_log, ssm_d, ssm_norm, ssm_out_w, kv_in_norm, w_dkv, ckv_norm, w_kr, w_uk, w_uv, k_nope_norm, k_rope_norm, w_dq, cq_norm, w_uq, q_nope_norm, q_rope_norm, mla_out_w):
    return _forward(_CFG, _TILES, x, p, positions, ln_mix, ln_mlp, mlp_w1, mlp_w2, ple_norm, ple_gate_w,
                    ple_proj_w, ssm_in_w, ssm_conv_w, ssm_conv_b, ssm_dt_bias, ssm_a_log, ssm_d, ssm_norm,
                    ssm_out_w, kv_in_norm, w_dkv, ckv_norm, w_kr, w_uk, w_uv, k_nope_norm, k_rope_norm,
                    w_dq, cq_norm, w_uq, q_nope_norm, q_rope_norm, mla_out_w)
```

```python
import functools
import math
from typing import NamedTuple

import jax
import jax.numpy as jnp
from jax import lax
from jax.experimental import pallas as pl
from jax.experimental.pallas import tpu as pltpu

F32 = jnp.float32
BF16 = jnp.bfloat16
EPS = 1e-6

LANES = 128
SUBLANES = 8
SSD_CHUNK = 128
HEADDIM = 64
SSM_STATE = 128
CONV_W = 4
CONV_HIST = 8
CONV_STRIDE = 4
ROW_CHUNK = 256
MASK_CHUNK_SHIFT = 6
QK_NOPE = 128
QK_ROPE = 64
V_HEAD = 128
QK_PAD = 256
V_AUG = 144
ROPE_THETA = 10000.0
LOG2E = math.log2(math.e)
MIB = 1024 * 1024


class _Cfg(NamedTuple):
    batch: int
    seq: int
    d_model: int
    d_ff: int
    ple_dim: int
    d_inner: int
    ssm_groups: int
    mla_heads: int
    q_lora: int
    kv_lora: int


class _Tiles(NamedTuple):
    tm_proj: int
    tm_out: int
    tn_proj: int
    tn_in: int
    tm_mlp: int
    tf_mlp: int
    tm_ple: int
    tm_mla: int
    t_attn: int
    attn_heads: int
    up_heads: int
    vmem_mb: int


def _cparams(sem, vmem_mb):
    return pltpu.CompilerParams(dimension_semantics=sem, vmem_limit_bytes=vmem_mb * MIB)


def _rms(x, g):
    return x * lax.rsqrt(jnp.mean(x * x, axis=-1, keepdims=True) + EPS) * g


def _sigmoid(x):
    return 1.0 / (1.0 + jnp.exp2(x * (-LOG2E)))


def _softplus(x):
    return jnp.maximum(x, 0.0) + jnp.log1p(jnp.exp(-jnp.abs(x)))


def _dot(a, b):
    return jnp.dot(a, b, preferred_element_type=F32)


def _dot_nt(a, b):
    return lax.dot_general(a, b, (((1,), (1,)), ((), ())), preferred_element_type=F32)


def _split3(v):
    hi = v.astype(BF16)
    r = v - hi.astype(F32)
    mid = r.astype(BF16)
    lo = (r - mid.astype(F32)).astype(BF16)
    return hi, mid, lo


def _inproj_kernel(h_ref, g_ref, w_ref, wdt_ref, zx_ref, dt_ref, hn_ref, *, rc):
    @pl.when(pl.program_id(1) == 0)
    def _():
        w = w_ref[...].astype(BF16)
        for r in range(h_ref.shape[0] // rc):
            rows = slice(r * rc, (r + 1) * rc)
            hn = _rms(h_ref[rows, :], g_ref[...]).astype(BF16)
            hn_ref[rows, :] = hn
            dt_ref[rows, :] = _dot_nt(hn, wdt_ref[...])
            zx_ref[rows, :] = _dot_nt(hn, w).astype(zx_ref.dtype)

    @pl.when(pl.program_id(1) > 0)
    def _():
        zx_ref[...] = _dot_nt(hn_ref[...], w_ref[...].astype(BF16)).astype(zx_ref.dtype)


def _inproj(h, gain, w_in_t, layer, n, w_dt, t):
    m, k = h.shape
    tm, tn = t.tm_proj, t.tn_in
    return pl.pallas_call(
        functools.partial(_inproj_kernel, rc=min(ROW_CHUNK, tm)),
        grid=(m // tm, n // tn),
        in_specs=[
            pl.BlockSpec((tm, k), lambda i, j: (i, 0)),
            pl.BlockSpec((1, k), lambda i, j: (0, 0)),
            pl.BlockSpec((None, tn, k), lambda i, j: (layer, j, 0)),
            pl.BlockSpec((LANES, k), lambda i, j: (0, 0)),
        ],
        out_specs=[
            pl.BlockSpec((tm, tn), lambda i, j: (i, j)),
            pl.BlockSpec((tm, LANES), lambda i, j: (i, 0)),
        ],
        out_shape=[jax.ShapeDtypeStruct((m, n), BF16), jax.ShapeDtypeStruct((m, LANES), F32)],
        scratch_shapes=[pltpu.VMEM((tm, k), BF16)],
        compiler_params=_cparams(("parallel", "arbitrary"), t.vmem_mb),
        name="ssm_in_proj",
    )(h, gain, w_in_t, w_dt)


def _ssd_kernel(z_ref, x_ref, b_ref, c_ref, dt_ref, cwb_ref, dtb_ref, alog_ref, dfull_ref, ng_ref, e_ref,
                out_ref, buf_ref, xc_ref, state_ref, *, d_inner, gn, groups):
    ln = SSD_CHUNK
    cdim = d_inner + 2 * gn
    nslab = cdim // LANES
    sl_b = d_inner // LANES
    sl_c = (d_inner + gn) // LANES
    heads = d_inner // HEADDIM
    hg = heads // groups
    gw = hg * HEADDIM
    c = pl.program_id(1)

    @pl.when(c == 0)
    def _():
        buf_ref[:, 0:CONV_HIST, :] = jnp.zeros((nslab, CONV_HIST, LANES), F32)
        state_ref[...] = jnp.zeros_like(state_ref)

    @pl.when(c > 0)
    def _():
        buf_ref[:, 0:CONV_HIST, :] = buf_ref[:, ln:ln + CONV_HIST, :]

    for s in range(nslab):
        lo = s * LANES
        if lo < d_inner:
            src = x_ref[:, lo:lo + LANES]
        elif lo < d_inner + gn:
            src = b_ref[:, lo - d_inner:lo - d_inner + LANES]
        else:
            src = c_ref[:, lo - d_inner - gn:lo - d_inner - gn + LANES]
        buf_ref[s, CONV_HIST:CONV_HIST + ln, :] = src.astype(F32)

    rows_per_blk = SUBLANES * CONV_STRIDE

    def conv_slab(s, carry):
        wb = cwb_ref[s]
        for blk in range(ln // rows_per_blk):
            for i in range(CONV_STRIDE):
                t0 = blk * rows_per_blk + i
                acc = wb[CONV_W:CONV_W + 1, :]
                for k in range(CONV_W):
                    start = CONV_HIST + t0 - (CONV_W - 1) + k
                    acc = acc + wb[k:k + 1, :] * buf_ref[s, pl.ds(start, SUBLANES, stride=CONV_STRIDE), :]
                xc_ref[s, pl.ds(t0, SUBLANES, stride=CONV_STRIDE), :] = acc + acc * jnp.tanh(acc)
        return carry

    lax.fori_loop(0, nslab, conv_slab, 0, unroll=8)

    dt = _softplus(dt_ref[...] + dtb_ref[...])
    a = dt * (-jnp.exp(alog_ref[...])) * LOG2E
    row = lax.broadcasted_iota(jnp.int32, (ln, ln), 0)
    col = lax.broadcasted_iota(jnp.int32, (ln, ln), 1)
    causal = row >= col
    tril = jnp.where(causal, 1.0, 0.0).astype(BF16)
    a_hi, a_mid, a_lo = _split3(a)
    acs = _dot(tril, a_hi) + _dot(tril, a_mid) + _dot(tril, a_lo)
    acs_t = acs.T
    last = acs[ln - 1:ln, :]
    eacs = jnp.exp2(acs)
    wend = jnp.exp2(last - acs) * dt
    lo_half = lax.broadcasted_iota(jnp.int32, (ln, LANES), 1) < HEADDIM

    def hi_mid(v):
        hi = v.astype(BF16)
        return jnp.concatenate([hi, (v - hi.astype(F32)).astype(BF16)], axis=1)

    per_head = jnp.concatenate([hi_mid(eacs), hi_mid(wend), hi_mid(dt)], axis=0)

    for g in range(groups):
        gcols = slice(g * gw, (g + 1) * gw)
        bg = xc_ref[sl_b + g]
        cg = xc_ref[sl_c + g].astype(BF16)
        cb = _dot_nt(cg, bg.astype(BF16))
        bt = bg.T.astype(BF16)
        eg = e_ref[:, gcols]
        xg = jnp.concatenate([xc_ref[g * (gw // LANES) + q] for q in range(gw // LANES)], axis=1)
        st = state_ref[:, gcols]
        lanes_g = _dot(per_head, eg)
        eacs_g = lanes_g[0:ln]
        y_off = _dot(cg, st.astype(BF16)) * eacs_g
        s_new = _dot(bt, (xg * lanes_g[ln:2 * ln]).astype(BF16))
        state_ref[:, gcols] = st * eacs_g[ln - 1:ln, :] + s_new
        xdt = xg * lanes_g[2 * ln:3 * ln]
        ys = []
        for p in range(hg // 2):
            pr = (g * hg) // 2 + p
            m_parts = []
            for hh in (2 * pr, 2 * pr + 1):
                a_row = acs_t[hh:hh + 1, :]
                a_col = jnp.broadcast_to(acs[:, hh:hh + 1], (ln, ln))
                dec = jnp.exp2(jnp.where(causal, a_col - a_row, -jnp.inf))
                m_parts.append((cb * dec).astype(BF16))
            x2 = xdt[:, p * LANES:(p + 1) * LANES]
            xa = jnp.where(lo_half, x2, 0.0).astype(BF16)
            xb = jnp.where(lo_half, 0.0, x2).astype(BF16)
            ys.append(_dot(jnp.concatenate(m_parts, axis=1), jnp.concatenate([xa, xb], axis=0)))
        yg = jnp.concatenate(ys, axis=1) + y_off + dfull_ref[:, gcols] * xg
        zg = z_ref[:, gcols].astype(F32)
        hz = 0.5 * zg
        yg = yg * (hz + hz * jnp.tanh(hz))
        out_ref[:, gcols] = _rms(yg, ng_ref[:, gcols]).astype(out_ref.dtype)


def _ssd(zx, dt_raw, conv_wb, dt_bias, a_log, d_full, norm_g, cfg, t):
    ln = SSD_CHUNK
    d_inner = cfg.d_inner
    gn = cfg.ssm_groups * SSM_STATE
    cdim = d_inner + 2 * gn
    nslab = cdim // LANES
    nc = cfg.seq // ln
    m = cfg.batch * cfg.seq
    b_blk = (2 * d_inner) // gn
    row = lambda b, c: b * nc + c
    kern = functools.partial(_ssd_kernel, d_inner=d_inner, gn=gn, groups=cfg.ssm_groups)
    head_lanes = (jnp.arange(d_inner)[None, :] // HEADDIM == jnp.arange(LANES)[:, None]).astype(BF16)
    head_lanes = jnp.concatenate([head_lanes, head_lanes], axis=0)
    const = lambda b, c: (0, 0)
    return pl.pallas_call(
        kern,
        grid=(cfg.batch, nc),
        in_specs=[
            pl.BlockSpec((ln, d_inner), lambda b, c: (row(b, c), 0)),
            pl.BlockSpec((ln, d_inner), lambda b, c: (row(b, c), 1)),
            pl.BlockSpec((ln, gn), lambda b, c: (row(b, c), b_blk)),
            pl.BlockSpec((ln, gn), lambda b, c: (row(b, c), b_blk + 1)),
            pl.BlockSpec((ln, LANES), lambda b, c: (row(b, c), 0)),
            pl.BlockSpec((nslab, SUBLANES, LANES), lambda b, c: (0, 0, 0)),
            pl.BlockSpec((1, LANES), const),
            pl.BlockSpec((1, LANES), const),
            pl.BlockSpec((1, d_inner), const),
            pl.BlockSpec((1, d_inner), const),
            pl.BlockSpec((2 * LANES, d_inner), const),
        ],
        out_specs=pl.BlockSpec((ln, d_inner), lambda b, c: (row(b, c), 0)),
        out_shape=jax.ShapeDtypeStruct((m, d_inner), BF16),
        scratch_shapes=[
            pltpu.VMEM((nslab, CONV_HIST + ln, LANES), F32),
            pltpu.VMEM((nslab, ln, LANES), F32),
            pltpu.VMEM((SSM_STATE, d_inner), F32),
        ],
        compiler_params=_cparams(("parallel", "arbitrary"), t.vmem_mb),
        name="ssm_scan",
    )(zx, zx, zx, zx, dt_raw, conv_wb, dt_bias, a_log, d_full, norm_g, head_lanes)


def _matmul_res_kernel(a_ref, w_ref, r_ref, o_ref, *, tn):
    a = a_ref[...]
    for j in range(o_ref.shape[1] // tn):
        cols = slice(j * tn, (j + 1) * tn)
        o_ref[:, cols] = r_ref[:, cols] + _dot(a, w_ref[:, cols])


def _matmul_res(a, w, layer, res, t, name):
    m, k = a.shape
    n = w.shape[2]
    tm, tn = t.tm_out, min(t.tn_proj, n)
    return pl.pallas_call(
        functools.partial(_matmul_res_kernel, tn=tn),
        grid=(m // tm,),
        in_specs=[
            pl.BlockSpec((tm, k), lambda i: (i, 0)),
            pl.BlockSpec((None, k, n), lambda i: (layer, 0, 0), pipeline_mode=pl.Buffered(1)),
            pl.BlockSpec((tm, n), lambda i: (i, 0)),
        ],
        out_specs=pl.BlockSpec((tm, n), lambda i: (i, 0)),
        out_shape=jax.ShapeDtypeStruct((m, n), F32),
        compiler_params=_cparams(("parallel",), t.vmem_mb),
        name=name,
    )(a, w, res)


def _mlp_kernel(h_ref, g_ref, w1_ref, w2_ref, o_ref, hn_ref, *, rc):
    def ff(hn, w1, w2):
        a = jnp.maximum(_dot(hn, w1), 0.0)
        return _dot((a * a).astype(BF16), w2)

    @pl.when(pl.program_id(1) == 0)
    def _():
        w1 = w1_ref[...].astype(BF16)
        w2 = w2_ref[...].astype(BF16)
        for r in range(h_ref.shape[0] // rc):
            rows = slice(r * rc, (r + 1) * rc)
            h = h_ref[rows, :]
            hn = _rms(h, g_ref[...]).astype(BF16)
            hn_ref[rows, :] = hn
            o_ref[rows, :] = h + ff(hn, w1, w2)

    @pl.when(pl.program_id(1) > 0)
    def _():
        o_ref[...] += ff(hn_ref[...], w1_ref[...].astype(BF16), w2_ref[...].astype(BF16))


def _mlp(h, gain, w1, w2, layer, t):
    m, d = h.shape
    f = w1.shape[2]
    tm, tf = t.tm_mlp, t.tf_mlp
    return pl.pallas_call(
        functools.partial(_mlp_kernel, rc=min(ROW_CHUNK, tm)),
        grid=(m // tm, f // tf),
        in_specs=[
            pl.BlockSpec((tm, d), lambda i, j: (i, 0)),
            pl.BlockSpec((1, d), lambda i, j: (0, 0)),
            pl.BlockSpec((None, d, tf), lambda i, j: (layer, 0, j)),
            pl.BlockSpec((None, tf, d), lambda i, j: (layer, j, 0)),
        ],
        out_specs=pl.BlockSpec((tm, d), lambda i, j: (i, 0)),
        out_shape=jax.ShapeDtypeStruct((m, d), F32),
        scratch_shapes=[pltpu.VMEM((tm, d), BF16)],
        compiler_params=_cparams(("parallel", "arbitrary"), t.vmem_mb),
        name="sq_relu_mlp",
    )(h, gain, w1, w2)


def _ple_kernel(h_ref, g_ref, wg_ref, p_ref, wp_ref, o_ref, *, tn, rc):
    for r in range(h_ref.shape[0] // rc):
        rows = slice(r * rc, (r + 1) * rc)
        hn = _rms(h_ref[rows, :], g_ref[...]).astype(BF16)
        pb = p_ref[rows, :].astype(BF16)
        for j in range(h_ref.shape[1] // tn):
            cols = slice(j * tn, (j + 1) * tn)
            gate = _sigmoid(_dot(hn, wg_ref[:, cols]))
            o_ref[rows, cols] = h_ref[rows, cols] + gate * _dot(pb, wp_ref[:, cols])


def _ple(h, gain, wg, p, layer, wp, t):
    m, d = h.shape
    pd = p.shape[2]
    tm = t.tm_ple
    tn = 512 if d % 512 == 0 else LANES
    return pl.pallas_call(
        functools.partial(_ple_kernel, tn=tn, rc=min(ROW_CHUNK, tm)),
        grid=(m // tm,),
        in_specs=[
            pl.BlockSpec((tm, d), lambda i: (i, 0)),
            pl.BlockSpec((1, d), lambda i: (0, 0)),
            pl.BlockSpec((d, d), lambda i: (0, 0)),
            pl.BlockSpec((None, tm, pd), lambda i: (layer, i, 0)),
            pl.BlockSpec((pd, d), lambda i: (0, 0)),
        ],
        out_specs=pl.BlockSpec((tm, d), lambda i: (i, 0)),
        out_shape=jax.ShapeDtypeStruct((m, d), F32),
        compiler_params=_cparams(("parallel",), t.vmem_mb),
        name="per_layer_input",
    )(h, gain, wg, p, wp)


def _rope(x, cos_f, sin_s):
    lane = lax.broadcasted_iota(jnp.int32, x.shape, 1)
    half = QK_ROPE // 2
    rot = jnp.where(lane < half, pltpu.roll(x, LANES - half, 1), pltpu.roll(x, half, 1))
    return x * cos_f + rot * sin_s


def _rms_rope_part(x, g_pad):
    ms = jnp.sum(x * x, axis=-1, keepdims=True) * (1.0 / QK_ROPE)
    return x * lax.rsqrt(ms + EPS) * g_pad


def _mla_a_kernel(h_ref, gkv_ref, gq_ref, wdkv_ref, wkr_ref, wdq_ref, ckvn_ref, krn_ref, cqn_ref,
                  pos_ref, fr_ref, ckv_ref, cq_ref, kr_ref, cos_ref, sin_ref):
    rc = min(ROW_CHUNK, h_ref.shape[0])
    krs = []
    for r in range(h_ref.shape[0] // rc):
        rows = slice(r * rc, (r + 1) * rc)
        h = h_ref[rows, :]
        y = h * lax.rsqrt(jnp.mean(h * h, axis=-1, keepdims=True) + EPS)
        hkv = (y * gkv_ref[...]).astype(BF16)
        hq = (y * gq_ref[...]).astype(BF16)
        ckv_ref[rows, :] = _rms(_dot(hkv, wdkv_ref[...]), ckvn_ref[...]).astype(BF16)
        cq_ref[rows, :] = _rms(_dot(hq, wdq_ref[...]), cqn_ref[...]).astype(BF16)
        krs.append(_dot(hkv, wkr_ref[...]))
    kr_raw = jnp.concatenate(krs, axis=0)
    ang = fr_ref[...] * pos_ref[...].astype(F32)
    cos_t = jnp.cos(ang)
    sin_t = jnp.sin(ang)
    cos_ref[...] = cos_t
    sin_ref[...] = sin_t
    pad = jnp.zeros((LANES - QK_ROPE, ang.shape[1]), F32)
    cos_f = jnp.concatenate([cos_t, cos_t, pad], axis=0).T
    sin_s = jnp.concatenate([-sin_t, sin_t, pad], axis=0).T
    kr = _rms_rope_part(kr_raw, krn_ref[...])
    kr_ref[...] = _rope(kr, cos_f, sin_s).astype(BF16)


def _mla_a(h, g_kv, g_q, w_dkv, w_kr, w_dq, ckv_n, kr_n, cq_n, pos, freqs, t):
    m, d = h.shape
    kvl = w_dkv.shape[1]
    ql = w_dq.shape[1]
    tm = t.tm_mla
    rowblk = lambda w: pl.BlockSpec((tm, w), lambda i: (i, 0))
    full = lambda a: pl.BlockSpec(a.shape, lambda i: (0, 0))
    colblk = pl.BlockSpec((QK_ROPE // 2, tm), lambda i: (0, i))
    return pl.pallas_call(
        _mla_a_kernel,
        grid=(m // tm,),
        in_specs=[rowblk(d), full(g_kv), full(g_q), full(w_dkv), full(w_kr), full(w_dq), full(ckv_n),
                  full(kr_n), full(cq_n), pl.BlockSpec((1, tm), lambda i: (0, i)), full(freqs)],
        out_specs=[rowblk(kvl), rowblk(ql), rowblk(LANES), colblk, colblk],
        out_shape=[jax.ShapeDtypeStruct((m, kvl), BF16), jax.ShapeDtypeStruct((m, ql), BF16),
                   jax.ShapeDtypeStruct((m, LANES), BF16), jax.ShapeDtypeStruct((QK_ROPE // 2, m), F32),
                   jax.ShapeDtypeStruct((QK_ROPE // 2, m), F32)],
        compiler_params=_cparams(("parallel",), t.vmem_mb),
        name="mla_down_proj",
    )(h, g_kv, g_q, w_dkv, w_kr, w_dq, ckv_n, kr_n, cq_n, pos, freqs)


def _mla_b_kernel(ckv_ref, cq_ref, kr_ref, cos_ref, sin_ref, wuk_ref, wuvt_ref, wuqt_ref, knn_ref, qnn_ref,
                  qrn_ref, q_ref, k_ref, vt_ref, *, scale, heads, hblk):
    ckv = ckv_ref[...]
    cq = cq_ref[...]
    kr = kr_ref[...]
    cos_t = cos_ref[...]
    sin_t = sin_ref[...]
    half = QK_ROPE // 2
    g_nope = qnn_ref[...] * scale
    g_rope = qrn_ref[...] * scale
    tm = ckv.shape[0]
    for h0 in range(0, heads, hblk):
        kn_blk = _dot(ckv, wuk_ref[:, h0 * QK_NOPE:(h0 + hblk) * QK_NOPE])
        vt_blk = _dot_nt(wuvt_ref[h0 * V_HEAD:(h0 + hblk) * V_HEAD, :], ckv)
        qt_blk = _dot_nt(wuqt_ref[h0 * QK_PAD:(h0 + hblk) * QK_PAD, :], cq)
        for i in range(hblk):
            h = h0 + i
            kn = _rms(kn_blk[:, i * QK_NOPE:(i + 1) * QK_NOPE], knn_ref[...])
            k_ref[0, h, :, 0:QK_NOPE] = kn.astype(BF16)
            k_ref[0, h, :, QK_NOPE:QK_PAD] = kr
            vt_ref[0, h, 0, 0:V_HEAD, :] = vt_blk[i * V_HEAD:(i + 1) * V_HEAD].astype(BF16)
            vt_ref[0, h, 0, V_HEAD:V_AUG, :] = jnp.ones((V_AUG - V_HEAD, tm), BF16)
            qt = qt_blk[i * QK_PAD:(i + 1) * QK_PAD]
            qn = qt[0:QK_NOPE]
            qn = qn * lax.rsqrt(jnp.mean(qn * qn, axis=0, keepdims=True) + EPS) * g_nope
            qr = qt[QK_NOPE:QK_NOPE + QK_ROPE]
            qr = qr * lax.rsqrt(jnp.mean(qr * qr, axis=0, keepdims=True) + EPS) * g_rope
            x1 = qr[0:half]
            x2 = qr[half:QK_ROPE]
            q_ref[0, h, 0, 0:QK_NOPE, :] = qn.astype(BF16)
            q_ref[0, h, 0, QK_NOPE:QK_NOPE + half, :] = (x1 * cos_t - x2 * sin_t).astype(BF16)
            q_ref[0, h, 0, QK_NOPE + half:QK_NOPE + QK_ROPE, :] = (x2 * cos_t + x1 * sin_t).astype(BF16)
            q_ref[0, h, 0, QK_NOPE + QK_ROPE:QK_PAD, :] = jnp.zeros((QK_PAD - QK_NOPE - QK_ROPE, tm), BF16)


def _mla_b(ckv, cq, kr, cos_t, sin_t, w_uk, w_uvt, w_uqt, kn_n, qn_n, qr_n, cfg, t):
    m, kvl = ckv.shape
    ql = cq.shape[1]
    hds = cfg.mla_heads
    tm = t.t_attn
    ns = cfg.seq // tm
    scale = (QK_NOPE + QK_ROPE) ** -0.5 * LOG2E
    rowblk = lambda w: pl.BlockSpec((tm, w), lambda i: (i, 0))
    full = lambda a: pl.BlockSpec(a.shape, lambda i: (0, 0))
    colblk = pl.BlockSpec((QK_ROPE // 2, tm), lambda i: (0, i))
    return pl.pallas_call(
        functools.partial(_mla_b_kernel, scale=scale, heads=hds, hblk=t.up_heads),
        grid=(m // tm,),
        in_specs=[rowblk(kvl), rowblk(ql), rowblk(LANES), colblk, colblk,
                  full(w_uk), full(w_uvt), full(w_uqt), full(kn_n), full(qn_n), full(qr_n)],
        out_specs=[
            pl.BlockSpec((1, hds, 1, QK_PAD, tm), lambda i: (i // ns, 0, i % ns, 0, 0)),
            pl.BlockSpec((1, hds, tm, QK_PAD), lambda i: (i // ns, 0, i % ns, 0)),
            pl.BlockSpec((1, hds, 1, V_AUG, tm), lambda i: (i // ns, 0, i % ns, 0, 0)),
        ],
        out_shape=[jax.ShapeDtypeStruct((cfg.batch, hds, ns, QK_PAD, tm), BF16),
                   jax.ShapeDtypeStruct((cfg.batch, hds, cfg.seq, QK_PAD), BF16),
                   jax.ShapeDtypeStruct((cfg.batch, hds, ns, V_AUG, tm), BF16)],
        compiler_params=_cparams(("parallel",), t.vmem_mb),
        name="mla_up_proj",
    )(ckv, cq, kr, cos_t, sin_t, w_uk, w_uvt, w_uqt, kn_n, qn_n, qr_n)


def _attn_kernel(q_ref, k_ref, vt_ref, o_ref, m_ref, acc_ref, sa_ref, sb_ref, bma_ref, bmb_ref, *, tq, hb):
    qi = pl.program_id(2)
    tk = tq // 2
    m_ref[...] = jnp.full_like(m_ref, -jnp.inf)
    acc_ref[...] = jnp.zeros_like(acc_ref)

    def visible_mask(k0):
        krow = k0 + lax.broadcasted_iota(jnp.int32, (tk, tq), 0)
        qcol = qi * tq + lax.broadcasted_iota(jnp.int32, (tk, tq), 1)
        return (krow >> MASK_CHUNK_SHIFT) <= (qcol >> MASK_CHUNK_SHIFT)

    def qk_stage(j, half, s_ref, bm_ref, masked):
        k0 = j * tq + half * tk
        if masked:
            visible = visible_mask(k0)
        for hh in range(hb):
            k = k_ref[0, hh, pl.ds(pl.multiple_of(k0, tk), tk), :]
            s = _dot(k, q_ref[0, hh, 0])
            if masked:
                s = jnp.where(visible, s, -jnp.inf)
            s_ref[hh] = s
            bm_ref[hh] = jnp.max(s, axis=0, keepdims=True)

    def sm_stage(j, half, s_ref, bm_ref, masked=False):
        if masked:
            visible = visible_mask(j * tq + half * tk)
        for hh in range(hb):
            m_prev = m_ref[hh]
            if masked:
                s = jnp.where(visible, s_ref[hh], -jnp.inf)
                m_new = jnp.maximum(m_prev, jnp.max(s, axis=0, keepdims=True))
            else:
                s = s_ref[hh]
                m_new = jnp.maximum(m_prev, bm_ref[hh])
            p = jnp.exp2(s - m_new)
            alpha = jnp.exp2(m_prev - m_new)
            vt = vt_ref[0, hh, j, :, half * tk:(half + 1) * tk]
            acc_ref[hh] = acc_ref[hh] * alpha + _dot(vt, p.astype(BF16))
            m_ref[hh] = m_new

    qk_stage(0, 0, sa_ref, bma_ref, False)

    def body(j, carry):
        qk_stage(j, 1, sb_ref, bmb_ref, False)
        sm_stage(j, 0, sa_ref, bma_ref)
        qk_stage(j + 1, 0, sa_ref, bma_ref, False)
        sm_stage(j, 1, sb_ref, bmb_ref)
        return carry

    lax.fori_loop(0, qi, body, 0)
    k1 = qi * tq + tk
    hi_q = slice(tk, tq)
    krow = k1 + lax.broadcasted_iota(jnp.int32, (tk, tk), 0)
    qcol = k1 + lax.broadcasted_iota(jnp.int32, (tk, tk), 1)
    visible_q = (krow >> MASK_CHUNK_SHIFT) <= (qcol >> MASK_CHUNK_SHIFT)
    for hh in range(hb):
        k = k_ref[0, hh, pl.ds(pl.multiple_of(k1, tk), tk), :]
        s = _dot(k, q_ref[0, hh, 0, :, hi_q])
        sb_ref[hh, :, hi_q] = jnp.where(visible_q, s, -jnp.inf)
    sm_stage(qi, 0, sa_ref, bma_ref, masked=True)
    for hh in range(hb):
        s = sb_ref[hh, :, hi_q]
        m_prev = m_ref[hh, :, hi_q]
        m_new = jnp.maximum(m_prev, jnp.max(s, axis=0, keepdims=True))
        p = jnp.exp2(s - m_new)
        alpha = jnp.exp2(m_prev - m_new)
        vt = vt_ref[0, hh, qi, :, hi_q]
        acc_ref[hh, :, hi_q] = acc_ref[hh, :, hi_q] * alpha + _dot(vt, p.astype(BF16))
        m_ref[hh, :, hi_q] = m_new
    for hh in range(hb):
        o = acc_ref[hh, 0:V_HEAD, :] / acc_ref[hh, V_HEAD:V_HEAD + 1, :]
        o_ref[0, :, hh * V_HEAD:(hh + 1) * V_HEAD] = o.T.astype(o_ref.dtype)


def _attn(q, k, vt, cfg, t):
    tq = t.t_attn
    hb = t.attn_heads
    hds = cfg.mla_heads
    nq = cfg.seq // tq
    return pl.pallas_call(
        functools.partial(_attn_kernel, tq=tq, hb=hb),
        grid=(cfg.batch, hds // hb, nq),
        in_specs=[
            pl.BlockSpec((1, hb, 1, QK_PAD, tq), lambda b, h, i: (b, h, i, 0, 0)),
            pl.BlockSpec((1, hb, cfg.seq, QK_PAD), lambda b, h, i: (b, h, 0, 0)),
            pl.BlockSpec((1, hb, nq, V_AUG, tq), lambda b, h, i: (b, h, 0, 0, 0)),
        ],
        out_specs=pl.BlockSpec((1, tq, hb * V_HEAD), lambda b, h, i: (b, i, h)),
        out_shape=jax.ShapeDtypeStruct((cfg.batch, cfg.seq, hds * V_HEAD), BF16),
        scratch_shapes=[pltpu.VMEM((hb, 1, tq), F32), pltpu.VMEM((hb, V_AUG, tq), F32),
                        pltpu.VMEM((hb, tq // 2, tq), F32), pltpu.VMEM((hb, tq // 2, tq), F32),
                        pltpu.VMEM((hb, 1, tq), F32), pltpu.VMEM((hb, 1, tq), F32)],
        compiler_params=_cparams(("parallel", "parallel", "arbitrary"), t.vmem_mb),
        name="mla_flash_attn",
    )(q, k, vt)


def _pad_lanes(v, width=LANES):
    v = v.reshape(1, -1)
    return jnp.pad(v, ((0, 0), (0, width - v.shape[1])))


def _forward(cfg, t, x, p, positions, ln_mix, ln_mlp, mlp_w1, mlp_w2, ple_norm, ple_gate_w, ple_proj_w,
             ssm_in_w, ssm_conv_w, ssm_conv_b, ssm_dt_bias, ssm_a_log, ssm_d, ssm_norm, ssm_out_w,
             kv_in_norm, w_dkv, ckv_norm, w_kr, w_uk, w_uv, k_nope_norm, k_rope_norm,
             w_dq, cq_norm, w_uq, q_nope_norm, q_rope_norm, mla_out_w):
    depth = ln_mix.shape[0]
    n_a = ssm_in_w.shape[0]
    m = cfg.batch * cfg.seq
    d = cfg.d_model
    hds = cfg.mla_heads
    row = lambda v: v.reshape(1, -1).astype(F32)
    bf = lambda w: w.astype(BF16)
    h = x.reshape(m, d)
    p2 = p.reshape(depth, m, cfg.ple_dim)

    kv_ready = False
    for i in range(depth):
        if i < n_a:
            j = i
            d_inner = cfg.d_inner
            cdim = d_inner + 2 * cfg.ssm_groups * SSM_STATE
            w_in_t = jnp.swapaxes(ssm_in_w, 1, 2)
            w_dt = bf(jnp.pad(w_in_t[j, d_inner + cdim:], ((0, LANES - d_inner // HEADDIM), (0, 0))))
            zx, dt_raw = _inproj(h, row(ln_mix[i]), w_in_t, j, d_inner + cdim, w_dt, t)
            cwb = jnp.concatenate([0.5 * ssm_conv_w[j], 0.5 * ssm_conv_b[j][None, :],
                                   jnp.zeros((SUBLANES - CONV_W - 1, cdim), F32)], axis=0)
            cwb = cwb.reshape(SUBLANES, cdim // LANES, LANES).transpose(1, 0, 2)
            yn = _ssd(zx, dt_raw, cwb, _pad_lanes(ssm_dt_bias[j]), _pad_lanes(ssm_a_log[j]),
                      row(jnp.repeat(ssm_d[j], HEADDIM)), row(ssm_norm[j]), cfg, t)
            h = _matmul_res(yn, bf(ssm_out_w), j, h, t, "ssm_out_proj")
        else:
            j = i - n_a
            if not kv_ready:
                kv_h = h
                kv_ready = True
            half = QK_ROPE // 2
            fr = ROPE_THETA ** (-jnp.arange(half, dtype=F32) / half)
            freqs = fr.reshape(half, 1)
            w_kr_p = bf(jnp.pad(w_kr, ((0, 0), (0, LANES - QK_ROPE))))
            assert depth - n_a == 1
            ckv, cq, kr, cos_t, sin_t = _mla_a(
                kv_h, row(kv_in_norm), row(ln_mix[i]), bf(w_dkv), w_kr_p, bf(w_dq[j]), row(ckv_norm),
                _pad_lanes(k_rope_norm), row(cq_norm[j]), positions.reshape(1, m), freqs, t)
            wq = w_uq[j].reshape(cfg.q_lora, hds, QK_NOPE + QK_ROPE)
            wq = jnp.pad(wq, ((0, 0), (0, 0), (0, QK_PAD - QK_NOPE - QK_ROPE))).reshape(cfg.q_lora, hds * QK_PAD)
            q, k, vt = _mla_b(ckv, cq, kr, cos_t, sin_t, bf(w_uk), bf(w_uv.T), bf(wq.T), row(k_nope_norm),
                              q_nope_norm[j].reshape(-1, 1), q_rope_norm[j].reshape(-1, 1), cfg, t)
            o = _attn(q, k, vt, cfg, t)
            h = _matmul_res(o.reshape(m, hds * V_HEAD), bf(mla_out_w), j, h, t, "mla_out_proj")
        h = _mlp(h, row(ln_mlp[i]), mlp_w1, mlp_w2, i, t)
        h = _ple(h, row(ple_norm[i]), bf(ple_gate_w[i]), p2, i, bf(ple_proj_w[i]), t)
    return h.reshape(cfg.batch, cfg.seq, d)


_CFG = _Cfg(batch=2, seq=4096, d_model=2048, d_ff=8192, ple_dim=256, d_inner=4096, ssm_groups=8,
            mla_heads=16, q_lora=512, kv_lora=512)
_TILES = _Tiles(tm_proj=1024, tm_out=512, tn_proj=512, tn_in=1024, tm_mlp=1024, tf_mlp=512, tm_ple=512, tm_mla=512, t_attn=512,
                attn_heads=4, up_heads=2, vmem_mb=60)


def kernel(x, p, positions, ln_mix, ln_mlp, mlp_w1, mlp_w2, ple_norm, ple_gate_w, ple_proj_w, ssm_in_w, ssm_conv_w, ssm_conv_b, ssm_dt_bias, ssm_a_log, ssm_d, ssm_norm, ssm_out_w, kv_in_norm, w_dkv, ckv_norm, w_kr, w_uk, w_uv, k_nope_norm, k_rope_norm, w_dq, cq_norm, w_uq, q_nope_norm, q_rope_norm, mla_out_w):
    return _forward(_CFG, _TILES, x, p, positions, ln_mix, ln_mlp, mlp_w1, mlp_w2, ple_norm, ple_gate_w,
                    ple_proj_w, ssm_in_w, ssm_conv_w, ssm_conv_b, ssm_dt_bias, ssm_a_log, ssm_d, ssm_norm,
                    ssm_out_w, kv_in_norm, w_dkv, ckv_norm, w_kr, w_uk, w_uv, k_nope_norm, k_rope_norm,
                    w_dq, cq_norm, w_uq, q_nope_norm, q_rope_norm, mla_out_w)
```

```python
import functools
import math
from typing import NamedTuple

import jax
import jax.numpy as jnp
from jax import lax
from jax.experimental import pallas as pl
from jax.experimental.pallas import tpu as pltpu

F32 = jnp.float32
BF16 = jnp.bfloat16
EPS = 1e-6

LANES = 128
SUBLANES = 8
SSD_CHUNK = 128
HEADDIM = 64
SSM_STATE = 128
CONV_W = 4
CONV_HIST = 8
CONV_STRIDE = 4
ROW_CHUNK = 256
MASK_CHUNK_SHIFT = 6
QK_NOPE = 128
QK_ROPE = 64
V_HEAD = 128
QK_PAD = 256
V_AUG = 144
ROPE_THETA = 10000.0
LOG2E = math.log2(math.e)
MIB = 1024 * 1024


class _Cfg(NamedTuple):
    batch: int
    seq: int
    d_model: int
    d_ff: int
    ple_dim: int
    d_inner: int
    ssm_groups: int
    mla_heads: int
    q_lora: int
    kv_lora: int


class _Tiles(NamedTuple):
    tm_proj: int
    tm_out: int
    tn_proj: int
    tn_in: int
    tm_mlp: int
    tf_mlp: int
    tm_ple: int
    tm_mla: int
    t_attn: int
    attn_heads: int
    up_heads: int
    vmem_mb: int


def _cparams(sem, vmem_mb):
    return pltpu.CompilerParams(dimension_semantics=sem, vmem_limit_bytes=vmem_mb * MIB)


def _rms(x, g):
    return x * lax.rsqrt(jnp.mean(x * x, axis=-1, keepdims=True) + EPS) * g


def _sigmoid(x):
    return 1.0 / (1.0 + jnp.exp2(x * (-LOG2E)))


def _softplus(x):
    return jnp.maximum(x, 0.0) + jnp.log1p(jnp.exp(-jnp.abs(x)))


def _dot(a, b):
    return jnp.dot(a, b, preferred_element_type=F32)


def _dot_nt(a, b):
    return lax.dot_general(a, b, (((1,), (1,)), ((), ())), preferred_element_type=F32)


def _split3(v):
    hi = v.astype(BF16)
    r = v - hi.astype(F32)
    mid = r.astype(BF16)
    lo = (r - mid.astype(F32)).astype(BF16)
    return hi, mid, lo


def _inproj_kernel(h_ref, g_ref, w_ref, wdt_ref, zx_ref, dt_ref, hn_ref, *, rc):
    @pl.when(pl.program_id(1) == 0)
    def _():
        w = w_ref[...].astype(BF16)
        for r in range(h_ref.shape[0] // rc):
            rows = slice(r * rc, (r + 1) * rc)
            hn = _rms(h_ref[rows, :], g_ref[...]).astype(BF16)
            hn_ref[rows, :] = hn
            dt_ref[rows, :] = _dot_nt(hn, wdt_ref[...])
            zx_ref[rows, :] = _dot_nt(hn, w).astype(zx_ref.dtype)

    @pl.when(pl.program_id(1) > 0)
    def _():
        zx_ref[...] = _dot_nt(hn_ref[...], w_ref[...].astype(BF16)).astype(zx_ref.dtype)


def _inproj(h, gain, w_in_t, layer, n, w_dt, t):
    m, k = h.shape
    tm, tn = t.tm_proj, t.tn_in
    return pl.pallas_call(
        functools.partial(_inproj_kernel, rc=min(ROW_CHUNK, tm)),
        grid=(m // tm, n // tn),
        in_specs=[
            pl.BlockSpec((tm, k), lambda i, j: (i, 0)),
            pl.BlockSpec((1, k), lambda i, j: (0, 0)),
            pl.BlockSpec((None, tn, k), lambda i, j: (layer, j, 0)),
            pl.BlockSpec((LANES, k), lambda i, j: (0, 0)),
        ],
        out_specs=[
            pl.BlockSpec((tm, tn), lambda i, j: (i, j)),
            pl.BlockSpec((tm, LANES), lambda i, j: (i, 0)),
        ],
        out_shape=[jax.ShapeDtypeStruct((m, n), BF16), jax.ShapeDtypeStruct((m, LANES), F32)],
        scratch_shapes=[pltpu.VMEM((tm, k), BF16)],
        compiler_params=_cparams(("parallel", "arbitrary"), t.vmem_mb),
        name="ssm_in_proj",
    )(h, gain, w_in_t, w_dt)


def _ssd_kernel(z_ref, x_ref, b_ref, c_ref, dt_ref, cwb_ref, dtb_ref, alog_ref, dfull_ref, ng_ref, e_ref,
                out_ref, buf_ref, xc_ref, state_ref, *, d_inner, gn, groups):
    ln = SSD_CHUNK
    cdim = d_inner + 2 * gn
    nslab = cdim // LANES
    sl_b = d_inner // LANES
    sl_c = (d_inner + gn) // LANES
    heads = d_inner // HEADDIM
    hg = heads // groups
    gw = hg * HEADDIM
    c = pl.program_id(1)

    @pl.when(c == 0)
    def _():
        buf_ref[:, 0:CONV_HIST, :] = jnp.zeros((nslab, CONV_HIST, LANES), F32)
        state_ref[...] = jnp.zeros_like(state_ref)

    @pl.when(c > 0)
    def _():
        buf_ref[:, 0:CONV_HIST, :] = buf_ref[:, ln:ln + CONV_HIST, :]

    for s in range(nslab):
        lo = s * LANES
        if lo < d_inner:
            src = x_ref[:, lo:lo + LANES]
        elif lo < d_inner + gn:
            src = b_ref[:, lo - d_inner:lo - d_inner + LANES]
        else:
            src = c_ref[:, lo - d_inner - gn:lo - d_inner - gn + LANES]
        buf_ref[s, CONV_HIST:CONV_HIST + ln, :] = src.astype(F32)

    rows_per_blk = SUBLANES * CONV_STRIDE

    def conv_slab(s, carry):
        wb = cwb_ref[s]
        for blk in range(ln // rows_per_blk):
            for i in range(CONV_STRIDE):
                t0 = blk * rows_per_blk + i
                acc = wb[CONV_W:CONV_W + 1, :]
                for k in range(CONV_W):
                    start = CONV_HIST + t0 - (CONV_W - 1) + k
                    acc = acc + wb[k:k + 1, :] * buf_ref[s, pl.ds(start, SUBLANES, stride=CONV_STRIDE), :]
                xc_ref[s, pl.ds(t0, SUBLANES, stride=CONV_STRIDE), :] = acc + acc * jnp.tanh(acc)
        return carry

    lax.fori_loop(0, nslab, conv_slab, 0, unroll=8)

    dt = _softplus(dt_ref[...] + dtb_ref[...])
    a = dt * (-jnp.exp(alog_ref[...])) * LOG2E
    row = lax.broadcasted_iota(jnp.int32, (ln, ln), 0)
    col = lax.broadcasted_iota(jnp.int32, (ln, ln), 1)
    causal = row >= col
    tril = jnp.where(causal, 1.0, 0.0).astype(BF16)
    a_hi, a_mid, a_lo = _split3(a)
    acs = _dot(tril, a_hi) + _dot(tril, a_mid) + _dot(tril, a_lo)
    acs_t = acs.T
    last = acs[ln - 1:ln, :]
    eacs = jnp.exp2(acs)
    wend = jnp.exp2(last - acs) * dt
    lo_half = lax.broadcasted_iota(jnp.int32, (ln, LANES), 1) < HEADDIM

    def hi_mid(v):
        hi = v.astype(BF16)
        return jnp.concatenate([hi, (v - hi.astype(F32)).astype(BF16)], axis=1)

    per_head = jnp.concatenate([hi_mid(eacs), hi_mid(wend), hi_mid(dt)], axis=0)

    for g in range(groups):
        gcols = slice(g * gw, (g + 1) * gw)
        bg = xc_ref[sl_b + g]
        cg = xc_ref[sl_c + g].astype(BF16)
        cb = _dot_nt(cg, bg.astype(BF16))
        bt = bg.T.astype(BF16)
        eg = e_ref[:, gcols]
        xg = jnp.concatenate([xc_ref[g * (gw // LANES) + q] for q in range(gw // LANES)], axis=1)
        st = state_ref[:, gcols]
        lanes_g = _dot(per_head, eg)
        eacs_g = lanes_g[0:ln]
        y_off = _dot(cg, st.astype(BF16)) * eacs_g
        s_new = _dot(bt, (xg * lanes_g[ln:2 * ln]).astype(BF16))
        state_ref[:, gcols] = st * eacs_g[ln - 1:ln, :] + s_new
        xdt = xg * lanes_g[2 * ln:3 * ln]
        ys = []
        for p in range(hg // 2):
            pr = (g * hg) // 2 + p
            m_parts = []
            for hh in (2 * pr, 2 * pr + 1):
                a_row = acs_t[hh:hh + 1, :]
                a_col = jnp.broadcast_to(acs[:, hh:hh + 1], (ln, ln))
                dec = jnp.exp2(jnp.where(causal, a_col - a_row, -jnp.inf))
                m_parts.append((cb * dec).astype(BF16))
            x2 = xdt[:, p * LANES:(p + 1) * LANES]
            xa = jnp.where(lo_half, x2, 0.0).astype(BF16)
            xb = jnp.where(lo_half, 0.0, x2).astype(BF16)
            ys.append(_dot(jnp.concatenate(m_parts, axis=1), jnp.concatenate([xa, xb], axis=0)))
        yg = jnp.concatenate(ys, axis=1) + y_off + dfull_ref[:, gcols] * xg
        zg = z_ref[:, gcols].astype(F32)
        hz = 0.5 * zg
        yg = yg * (hz + hz * jnp.tanh(hz))
        out_ref[:, gcols] = _rms(yg, ng_ref[:, gcols]).astype(out_ref.dtype)


def _ssd(zx, dt_raw, conv_wb, dt_bias, a_log, d_full, norm_g, cfg, t):
    ln = SSD_CHUNK
    d_inner = cfg.d_inner
    gn = cfg.ssm_groups * SSM_STATE
    cdim = d_inner + 2 * gn
    nslab = cdim // LANES
    nc = cfg.seq // ln
    m = cfg.batch * cfg.seq
    b_blk = (2 * d_inner) // gn
    row = lambda b, c: b * nc + c
    kern = functools.partial(_ssd_kernel, d_inner=d_inner, gn=gn, groups=cfg.ssm_groups)
    head_lanes = (jnp.arange(d_inner)[None, :] // HEADDIM == jnp.arange(LANES)[:, None]).astype(BF16)
    head_lanes = jnp.concatenate([head_lanes, head_lanes], axis=0)
    const = lambda b, c: (0, 0)
    return pl.pallas_call(
        kern,
        grid=(cfg.batch, nc),
        in_specs=[
            pl.BlockSpec((ln, d_inner), lambda b, c: (row(b, c), 0)),
            pl.BlockSpec((ln, d_inner), lambda b, c: (row(b, c), 1)),
            pl.BlockSpec((ln, gn), lambda b, c: (row(b, c), b_blk)),
            pl.BlockSpec((ln, gn), lambda b, c: (row(b, c), b_blk + 1)),
            pl.BlockSpec((ln, LANES), lambda b, c: (row(b, c), 0)),
            pl.BlockSpec((nslab, SUBLANES, LANES), lambda b, c: (0, 0, 0)),
            pl.BlockSpec((1, LANES), const),
            pl.BlockSpec((1, LANES), const),
            pl.BlockSpec((1, d_inner), const),
            pl.BlockSpec((1, d_inner), const),
            pl.BlockSpec((2 * LANES, d_inner), const),
        ],
        out_specs=pl.BlockSpec((ln, d_inner), lambda b, c: (row(b, c), 0)),
        out_shape=jax.ShapeDtypeStruct((m, d_inner), BF16),
        scratch_shapes=[
            pltpu.VMEM((nslab, CONV_HIST + ln, LANES), F32),
            pltpu.VMEM((nslab, ln, LANES), F32),
            pltpu.VMEM((SSM_STATE, d_inner), F32),
        ],
        compiler_params=_cparams(("parallel", "arbitrary"), t.vmem_mb),
        name="ssm_scan",
    )(zx, zx, zx, zx, dt_raw, conv_wb, dt_bias, a_log, d_full, norm_g, head_lanes)


def _matmul_res_kernel(a_ref, w_ref, r_ref, o_ref, *, tn):
    a = a_ref[...]
    for j in range(o_ref.shape[1] // tn):
        cols = slice(j * tn, (j + 1) * tn)
        o_ref[:, cols] = r_ref[:, cols] + _dot(a, w_ref[:, cols])


def _matmul_res(a, w, layer, res, t, name):
    m, k = a.shape
    n = w.shape[2]
    tm, tn = t.tm_out, min(t.tn_proj, n)
    return pl.pallas_call(
        functools.partial(_matmul_res_kernel, tn=tn),
        grid=(m // tm,),
        in_specs=[
            pl.BlockSpec((tm, k), lambda i: (i, 0)),
            pl.BlockSpec((None, k, n), lambda i: (layer, 0, 0), pipeline_mode=pl.Buffered(1)),
            pl.BlockSpec((tm, n), lambda i: (i, 0)),
        ],
        out_specs=pl.BlockSpec((tm, n), lambda i: (i, 0)),
        out_shape=jax.ShapeDtypeStruct((m, n), F32),
        compiler_params=_cparams(("parallel",), t.vmem_mb),
        name=name,
    )(a, w, res)


def _mlp_kernel(h_ref, g_ref, w1_ref, w2_ref, o_ref, hn_ref, *, rc):
    def ff(hn, w1, w2):
        a = jnp.maximum(_dot(hn, w1), 0.0)
        return _dot((a * a).astype(BF16), w2)

    @pl.when(pl.program_id(1) == 0)
    def _():
        w1 = w1_ref[...].astype(BF16)
        w2 = w2_ref[...].astype(BF16)
        for r in range(h_ref.shape[0] // rc):
            rows = slice(r * rc, (r + 1) * rc)
            h = h_ref[rows, :]
            hn = _rms(h, g_ref[...]).astype(BF16)
            hn_ref[rows, :] = hn
            o_ref[rows, :] = h + ff(hn, w1, w2)

    @pl.when(pl.program_id(1) > 0)
    def _():
        o_ref[...] += ff(hn_ref[...], w1_ref[...].astype(BF16), w2_ref[...].astype(BF16))


def _mlp(h, gain, w1, w2, layer, t):
    m, d = h.shape
    f = w1.shape[2]
    tm, tf = t.tm_mlp, t.tf_mlp
    return pl.pallas_call(
        functools.partial(_mlp_kernel, rc=min(ROW_CHUNK, tm)),
        grid=(m // tm, f // tf),
        in_specs=[
            pl.BlockSpec((tm, d), lambda i, j: (i, 0)),
            pl.BlockSpec((1, d), lambda i, j: (0, 0)),
            pl.BlockSpec((None, d, tf), lambda i, j: (layer, 0, j)),
            pl.BlockSpec((None, tf, d), lambda i, j: (layer, j, 0)),
        ],
        out_specs=pl.BlockSpec((tm, d), lambda i, j: (i, 0)),
        out_shape=jax.ShapeDtypeStruct((m, d), F32),
        scratch_shapes=[pltpu.VMEM((tm, d), BF16)],
        compiler_params=_cparams(("parallel", "arbitrary"), t.vmem_mb),
        name="sq_relu_mlp",
    )(h, gain, w1, w2)


def _ple_kernel(h_ref, g_ref, wg_ref, p_ref, wp_ref, o_ref, *, tn, rc):
    for r in range(h_ref.shape[0] // rc):
        rows = slice(r * rc, (r + 1) * rc)
        hn = _rms(h_ref[rows, :], g_ref[...]).astype(BF16)
        pb = p_ref[rows, :].astype(BF16)
        for j in range(h_ref.shape[1] // tn):
            cols = slice(j * tn, (j + 1) * tn)
            gate = _sigmoid(_dot(hn, wg_ref[:, cols]))
            o_ref[rows, cols] = h_ref[rows, cols] + gate * _dot(pb, wp_ref[:, cols])


def _ple(h, gain, wg, p, layer, wp, t):
    m, d = h.shape
    pd = p.shape[2]
    tm = t.tm_ple
    tn = 512 if d % 512 == 0 else LANES
    return pl.pallas_call(
        functools.partial(_ple_kernel, tn=tn, rc=min(ROW_CHUNK, tm)),
        grid=(m // tm,),
        in_specs=[
            pl.BlockSpec((tm, d), lambda i: (i, 0)),
            pl.BlockSpec((1, d), lambda i: (0, 0)),
            pl.BlockSpec((d, d), lambda i: (0, 0)),
            pl.BlockSpec((None, tm, pd), lambda i: (layer, i, 0)),
            pl.BlockSpec((pd, d), lambda i: (0, 0)),
        ],
        out_specs=pl.BlockSpec((tm, d), lambda i: (i, 0)),
        out_shape=jax.ShapeDtypeStruct((m, d), F32),
        compiler_params=_cparams(("parallel",), t.vmem_mb),
        name="per_layer_input",
    )(h, gain, wg, p, wp)


def _rope(x, cos_f, sin_s):
    lane = lax.broadcasted_iota(jnp.int32, x.shape, 1)
    half = QK_ROPE // 2
    rot = jnp.where(lane < half, pltpu.roll(x, LANES - half, 1), pltpu.roll(x, half, 1))
    return x * cos_f + rot * sin_s


def _rms_rope_part(x, g_pad):
    ms = jnp.sum(x * x, axis=-1, keepdims=True) * (1.0 / QK_ROPE)
    return x * lax.rsqrt(ms + EPS) * g_pad


def _mla_a_kernel(h_ref, gkv_ref, gq_ref, wdkv_ref, wkr_ref, wdq_ref, ckvn_ref, krn_ref, cqn_ref,
                  pos_ref, fr_ref, ckv_ref, cq_ref, kr_ref, cos_ref, sin_ref):
    rc = min(ROW_CHUNK, h_ref.shape[0])
    krs = []
    for r in range(h_ref.shape[0] // rc):
        rows = slice(r * rc, (r + 1) * rc)
        h = h_ref[rows, :]
        y = h * lax.rsqrt(jnp.mean(h * h, axis=-1, keepdims=True) + EPS)
        hkv = (y * gkv_ref[...]).astype(BF16)
        hq = (y * gq_ref[...]).astype(BF16)
        ckv_ref[rows, :] = _rms(_dot(hkv, wdkv_ref[...]), ckvn_ref[...]).astype(BF16)
        cq_ref[rows, :] = _rms(_dot(hq, wdq_ref[...]), cqn_ref[...]).astype(BF16)
        krs.append(_dot(hkv, wkr_ref[...]))
    kr_raw = jnp.concatenate(krs, axis=0)
    ang = fr_ref[...] * pos_ref[...].astype(F32)
    cos_t = jnp.cos(ang)
    sin_t = jnp.sin(ang)
    cos_ref[...] = cos_t
    sin_ref[...] = sin_t
    pad = jnp.zeros((LANES - QK_ROPE, ang.shape[1]), F32)
    cos_f = jnp.concatenate([cos_t, cos_t, pad], axis=0).T
    sin_s = jnp.concatenate([-sin_t, sin_t, pad], axis=0).T
    kr = _rms_rope_part(kr_raw, krn_ref[...])
    kr_ref[...] = _rope(kr, cos_f, sin_s).astype(BF16)


def _mla_a(h, g_kv, g_q, w_dkv, w_kr, w_dq, ckv_n, kr_n, cq_n, pos, freqs, t):
    m, d = h.shape
    kvl = w_dkv.shape[1]
    ql = w_dq.shape[1]
    tm = t.tm_mla
    rowblk = lambda w: pl.BlockSpec((tm, w), lambda i: (i, 0))
    full = lambda a: pl.BlockSpec(a.shape, lambda i: (0, 0))
    colblk = pl.BlockSpec((QK_ROPE // 2, tm), lambda i: (0, i))
    return pl.pallas_call(
        _mla_a_kernel,
        grid=(m // tm,),
        in_specs=[rowblk(d), full(g_kv), full(g_q), full(w_dkv), full(w_kr), full(w_dq), full(ckv_n),
                  full(kr_n), full(cq_n), pl.BlockSpec((1, tm), lambda i: (0, i)), full(freqs)],
        out_specs=[rowblk(kvl), rowblk(ql), rowblk(LANES), colblk, colblk],
        out_shape=[jax.ShapeDtypeStruct((m, kvl), BF16), jax.ShapeDtypeStruct((m, ql), BF16),
                   jax.ShapeDtypeStruct((m, LANES), BF16), jax.ShapeDtypeStruct((QK_ROPE // 2, m), F32),
                   jax.ShapeDtypeStruct((QK_ROPE // 2, m), F32)],
        compiler_params=_cparams(("parallel",), t.vmem_mb),
        name="mla_down_proj",
    )(h, g_kv, g_q, w_dkv, w_kr, w_dq, ckv_n, kr_n, cq_n, pos, freqs)


def _mla_b_kernel(ckv_ref, cq_ref, kr_ref, cos_ref, sin_ref, wuk_ref, wuvt_ref, wuqt_ref, knn_ref, qnn_ref,
                  qrn_ref, q_ref, k_ref, vt_ref, *, scale, heads, hblk):
    ckv = ckv_ref[...]
    cq = cq_ref[...]
    kr = kr_ref[...]
    cos_t = cos_ref[...]
    sin_t = sin_ref[...]
    half = QK_ROPE // 2
    g_nope = qnn_ref[...] * scale
    g_rope = qrn_ref[...] * scale
    tm = ckv.shape[0]
    for h0 in range(0, heads, hblk):
        kn_blk = _dot(ckv, wuk_ref[:, h0 * QK_NOPE:(h0 + hblk) * QK_NOPE])
        vt_blk = _dot_nt(wuvt_ref[h0 * V_HEAD:(h0 + hblk) * V_HEAD, :], ckv)
        qt_blk = _dot_nt(wuqt_ref[h0 * QK_PAD:(h0 + hblk) * QK_PAD, :], cq)
        for i in range(hblk):
            h = h0 + i
            kn = _rms(kn_blk[:, i * QK_NOPE:(i + 1) * QK_NOPE], knn_ref[...])
            k_ref[0, h, :, 0:QK_NOPE] = kn.astype(BF16)
            k_ref[0, h, :, QK_NOPE:QK_PAD] = kr
            vt_ref[0, h, 0, 0:V_HEAD, :] = vt_blk[i * V_HEAD:(i + 1) * V_HEAD].astype(BF16)
            vt_ref[0, h, 0, V_HEAD:V_AUG, :] = jnp.ones((V_AUG - V_HEAD, tm), BF16)
            qt = qt_blk[i * QK_PAD:(i + 1) * QK_PAD]
            qn = qt[0:QK_NOPE]
            qn = qn * lax.rsqrt(jnp.mean(qn * qn, axis=0, keepdims=True) + EPS) * g_nope
            qr = qt[QK_NOPE:QK_NOPE + QK_ROPE]
            qr = qr * lax.rsqrt(jnp.mean(qr * qr, axis=0, keepdims=True) + EPS) * g_rope
            x1 = qr[0:half]
            x2 = qr[half:QK_ROPE]
            q_ref[0, h, 0, 0:QK_NOPE, :] = qn.astype(BF16)
            q_ref[0, h, 0, QK_NOPE:QK_NOPE + half, :] = (x1 * cos_t - x2 * sin_t).astype(BF16)
            q_ref[0, h, 0, QK_NOPE + half:QK_NOPE + QK_ROPE, :] = (x2 * cos_t + x1 * sin_t).astype(BF16)
            q_ref[0, h, 0, QK_NOPE + QK_ROPE:QK_PAD, :] = jnp.zeros((QK_PAD - QK_NOPE - QK_ROPE, tm), BF16)


def _mla_b(ckv, cq, kr, cos_t, sin_t, w_uk, w_uvt, w_uqt, kn_n, qn_n, qr_n, cfg, t):
    m, kvl = ckv.shape
    ql = cq.shape[1]
    hds = cfg.mla_heads
    tm = t.t_attn
    ns = cfg.seq // tm
    scale = (QK_NOPE + QK_ROPE) ** -0.5 * LOG2E
    rowblk = lambda w: pl.BlockSpec((tm, w), lambda i: (i, 0))
    full = lambda a: pl.BlockSpec(a.shape, lambda i: (0, 0))
    colblk = pl.BlockSpec((QK_ROPE // 2, tm), lambda i: (0, i))
    return pl.pallas_call(
        functools.partial(_mla_b_kernel, scale=scale, heads=hds, hblk=t.up_heads),
        grid=(m // tm,),
        in_specs=[rowblk(kvl), rowblk(ql), rowblk(LANES), colblk, colblk,
                  full(w_uk), full(w_uvt), full(w_uqt), full(kn_n), full(qn_n), full(qr_n)],
        out_specs=[
            pl.BlockSpec((1, hds, 1, QK_PAD, tm), lambda i: (i // ns, 0, i % ns, 0, 0)),
            pl.BlockSpec((1, hds, tm, QK_PAD), lambda i: (i // ns, 0, i % ns, 0)),
            pl.BlockSpec((1, hds, 1, V_AUG, tm), lambda i: (i // ns, 0, i % ns, 0, 0)),
        ],
        out_shape=[jax.ShapeDtypeStruct((cfg.batch, hds, ns, QK_PAD, tm), BF16),
                   jax.ShapeDtypeStruct((cfg.batch, hds, cfg.seq, QK_PAD), BF16),
                   jax.ShapeDtypeStruct((cfg.batch, hds, ns, V_AUG, tm), BF16)],
        compiler_params=_cparams(("parallel",), t.vmem_mb),
        name="mla_up_proj",
    )(ckv, cq, kr, cos_t, sin_t, w_uk, w_uvt, w_uqt, kn_n, qn_n, qr_n)


def _mla_proj_kernel(h_ref, gkv_ref, gq_ref, wdkv_ref, wkr_ref, wdq_ref, ckvn_ref, krn_ref, cqn_ref, pos_ref, fr_ref,
                     wuk_ref, wuvt_ref, wuqt_ref, knn_ref, qnn_ref, qrn_ref, q_ref, k_ref, vt_ref,
                     ckv_ref, cq_ref, kr_ref, cos_ref, sin_ref, *, scale, heads, hblk):
    _mla_a_kernel(h_ref, gkv_ref, gq_ref, wdkv_ref, wkr_ref, wdq_ref, ckvn_ref, krn_ref, cqn_ref, pos_ref, fr_ref,
                  ckv_ref, cq_ref, kr_ref, cos_ref, sin_ref)
    _mla_b_kernel(ckv_ref, cq_ref, kr_ref, cos_ref, sin_ref, wuk_ref, wuvt_ref, wuqt_ref, knn_ref, qnn_ref, qrn_ref,
                  q_ref, k_ref, vt_ref, scale=scale, heads=heads, hblk=hblk)


def _mla_proj(h, g_kv, g_q, w_dkv, w_kr, w_dq, ckv_n, kr_n, cq_n, pos, freqs, w_uk, w_uvt, w_uqt, kn_n, qn_n, qr_n,
              cfg, t):
    m, d = h.shape
    kvl = w_dkv.shape[1]
    ql = w_dq.shape[1]
    hds = cfg.mla_heads
    tm = t.t_attn
    ns = cfg.seq // tm
    scale = (QK_NOPE + QK_ROPE) ** -0.5 * LOG2E
    full = lambda a: pl.BlockSpec(a.shape, lambda i: (0, 0))
    half = QK_ROPE // 2
    return pl.pallas_call(
        functools.partial(_mla_proj_kernel, scale=scale, heads=hds, hblk=t.up_heads),
        grid=(m // tm,),
        in_specs=[pl.BlockSpec((tm, d), lambda i: (i, 0)), full(g_kv), full(g_q), full(w_dkv), full(w_kr),
                  full(w_dq), full(ckv_n), full(kr_n), full(cq_n), pl.BlockSpec((1, tm), lambda i: (0, i)),
                  full(freqs), full(w_uk), full(w_uvt), full(w_uqt), full(kn_n), full(qn_n), full(qr_n)],
        out_specs=[
            pl.BlockSpec((1, hds, 1, QK_PAD, tm), lambda i: (i // ns, 0, i % ns, 0, 0)),
            pl.BlockSpec((1, hds, tm, QK_PAD), lambda i: (i // ns, 0, i % ns, 0)),
            pl.BlockSpec((1, hds, 1, V_AUG, tm), lambda i: (i // ns, 0, i % ns, 0, 0)),
        ],
        out_shape=[jax.ShapeDtypeStruct((cfg.batch, hds, ns, QK_PAD, tm), BF16),
                   jax.ShapeDtypeStruct((cfg.batch, hds, cfg.seq, QK_PAD), BF16),
                   jax.ShapeDtypeStruct((cfg.batch, hds, ns, V_AUG, tm), BF16)],
        scratch_shapes=[pltpu.VMEM((tm, kvl), BF16), pltpu.VMEM((tm, ql), BF16), pltpu.VMEM((tm, LANES), BF16),
                        pltpu.VMEM((half, tm), F32), pltpu.VMEM((half, tm), F32)],
        compiler_params=_cparams(("parallel",), t.vmem_mb),
        name="mla_proj",
    )(h, g_kv, g_q, w_dkv, w_kr, w_dq, ckv_n, kr_n, cq_n, pos, freqs, w_uk, w_uvt, w_uqt, kn_n, qn_n, qr_n)


def _attn_kernel(q_ref, k_ref, vt_ref, o_ref, m_ref, acc_ref, sa_ref, sb_ref, bma_ref, bmb_ref, *, tq, hb):
    qi = pl.program_id(2)
    tk = tq // 2
    m_ref[...] = jnp.full_like(m_ref, -jnp.inf)
    acc_ref[...] = jnp.zeros_like(acc_ref)

    def visible_mask(k0):
        krow = k0 + lax.broadcasted_iota(jnp.int32, (tk, tq), 0)
        qcol = qi * tq + lax.broadcasted_iota(jnp.int32, (tk, tq), 1)
        return (krow >> MASK_CHUNK_SHIFT) <= (qcol >> MASK_CHUNK_SHIFT)

    def qk_stage(j, half, s_ref, bm_ref, masked):
        k0 = j * tq + half * tk
        if masked:
            visible = visible_mask(k0)
        for hh in range(hb):
            k = k_ref[0, hh, pl.ds(pl.multiple_of(k0, tk), tk), :]
            s = _dot(k, q_ref[0, hh, 0])
            if masked:
                s = jnp.where(visible, s, -jnp.inf)
            s_ref[hh] = s
            bm_ref[hh] = jnp.max(s, axis=0, keepdims=True)

    def sm_stage(j, half, s_ref, bm_ref, masked=False):
        if masked:
            visible = visible_mask(j * tq + half * tk)
        for hh in range(hb):
            m_prev = m_ref[hh]
            if masked:
                s = jnp.where(visible, s_ref[hh], -jnp.inf)
                m_new = jnp.maximum(m_prev, jnp.max(s, axis=0, keepdims=True))
            else:
                s = s_ref[hh]
                m_new = jnp.maximum(m_prev, bm_ref[hh])
            p = jnp.exp2(s - m_new)
            alpha = jnp.exp2(m_prev - m_new)
            vt = vt_ref[0, hh, j, :, half * tk:(half + 1) * tk]
            acc_ref[hh] = acc_ref[hh] * alpha + _dot(vt, p.astype(BF16))
            m_ref[hh] = m_new

    qk_stage(0, 0, sa_ref, bma_ref, False)

    def body(j, carry):
        qk_stage(j, 1, sb_ref, bmb_ref, False)
        sm_stage(j, 0, sa_ref, bma_ref)
        qk_stage(j + 1, 0, sa_ref, bma_ref, False)
        sm_stage(j, 1, sb_ref, bmb_ref)
        return carry

    lax.fori_loop(0, qi, body, 0)
    k1 = qi * tq + tk
    hi_q = slice(tk, tq)
    krow = k1 + lax.broadcasted_iota(jnp.int32, (tk, tk), 0)
    qcol = k1 + lax.broadcasted_iota(jnp.int32, (tk, tk), 1)
    visible_q = (krow >> MASK_CHUNK_SHIFT) <= (qcol >> MASK_CHUNK_SHIFT)
    for hh in range(hb):
        k = k_ref[0, hh, pl.ds(pl.multiple_of(k1, tk), tk), :]
        s = _dot(k, q_ref[0, hh, 0, :, hi_q])
        sb_ref[hh, :, hi_q] = jnp.where(visible_q, s, -jnp.inf)
    sm_stage(qi, 0, sa_ref, bma_ref, masked=True)
    for hh in range(hb):
        s = sb_ref[hh, :, hi_q]
        m_prev = m_ref[hh, :, hi_q]
        m_new = jnp.maximum(m_prev, jnp.max(s, axis=0, keepdims=True))
        p = jnp.exp2(s - m_new)
        alpha = jnp.exp2(m_prev - m_new)
        vt = vt_ref[0, hh, qi, :, hi_q]
        acc_ref[hh, :, hi_q] = acc_ref[hh, :, hi_q] * alpha + _dot(vt, p.astype(BF16))
        m_ref[hh, :, hi_q] = m_new
    for hh in range(hb):
        o = acc_ref[hh, 0:V_HEAD, :] / acc_ref[hh, V_HEAD:V_HEAD + 1, :]
        o_ref[0, :, hh * V_HEAD:(hh + 1) * V_HEAD] = o.T.astype(o_ref.dtype)


def _attn(q, k, vt, cfg, t):
    tq = t.t_attn
    hb = t.attn_heads
    hds = cfg.mla_heads
    nq = cfg.seq // tq
    return pl.pallas_call(
        functools.partial(_attn_kernel, tq=tq, hb=hb),
        grid=(cfg.batch, hds // hb, nq),
        in_specs=[
            pl.BlockSpec((1, hb, 1, QK_PAD, tq), lambda b, h, i: (b, h, i, 0, 0)),
            pl.BlockSpec((1, hb, cfg.seq, QK_PAD), lambda b, h, i: (b, h, 0, 0)),
            pl.BlockSpec((1, hb, nq, V_AUG, tq), lambda b, h, i: (b, h, 0, 0, 0)),
        ],
        out_specs=pl.BlockSpec((1, tq, hb * V_HEAD), lambda b, h, i: (b, i, h)),
        out_shape=jax.ShapeDtypeStruct((cfg.batch, cfg.seq, hds * V_HEAD), BF16),
        scratch_shapes=[pltpu.VMEM((hb, 1, tq), F32), pltpu.VMEM((hb, V_AUG, tq), F32),
                        pltpu.VMEM((hb, tq // 2, tq), F32), pltpu.VMEM((hb, tq // 2, tq), F32),
                        pltpu.VMEM((hb, 1, tq), F32), pltpu.VMEM((hb, 1, tq), F32)],
        compiler_params=_cparams(("parallel", "parallel", "arbitrary"), t.vmem_mb),
        name="mla_flash_attn",
    )(q, k, vt)


def _pad_lanes(v, width=LANES):
    v = v.reshape(1, -1)
    return jnp.pad(v, ((0, 0), (0, width - v.shape[1])))


def _forward(cfg, t, x, p, positions, ln_mix, ln_mlp, mlp_w1, mlp_w2, ple_norm, ple_gate_w, ple_proj_w,
             ssm_in_w, ssm_conv_w, ssm_conv_b, ssm_dt_bias, ssm_a_log, ssm_d, ssm_norm, ssm_out_w,
             kv_in_norm, w_dkv, ckv_norm, w_kr, w_uk, w_uv, k_nope_norm, k_rope_norm,
             w_dq, cq_norm, w_uq, q_nope_norm, q_rope_norm, mla_out_w):
    depth = ln_mix.shape[0]
    n_a = ssm_in_w.shape[0]
    m = cfg.batch * cfg.seq
    d = cfg.d_model
    hds = cfg.mla_heads
    row = lambda v: v.reshape(1, -1).astype(F32)
    bf = lambda w: w.astype(BF16)
    h = x.reshape(m, d)
    p2 = p.reshape(depth, m, cfg.ple_dim)

    kv_ready = False
    for i in range(depth):
        if i < n_a:
            j = i
            d_inner = cfg.d_inner
            cdim = d_inner + 2 * cfg.ssm_groups * SSM_STATE
            w_in_t = jnp.swapaxes(ssm_in_w, 1, 2)
            w_dt = bf(jnp.pad(w_in_t[j, d_inner + cdim:], ((0, LANES - d_inner // HEADDIM), (0, 0))))
            zx, dt_raw = _inproj(h, row(ln_mix[i]), w_in_t, j, d_inner + cdim, w_dt, t)
            cwb = jnp.concatenate([0.5 * ssm_conv_w[j], 0.5 * ssm_conv_b[j][None, :],
                                   jnp.zeros((SUBLANES - CONV_W - 1, cdim), F32)], axis=0)
            cwb = cwb.reshape(SUBLANES, cdim // LANES, LANES).transpose(1, 0, 2)
            yn = _ssd(zx, dt_raw, cwb, _pad_lanes(ssm_dt_bias[j]), _pad_lanes(ssm_a_log[j]),
                      row(jnp.repeat(ssm_d[j], HEADDIM)), row(ssm_norm[j]), cfg, t)
            h = _matmul_res(yn, bf(ssm_out_w), j, h, t, "ssm_out_proj")
        else:
            j = i - n_a
            if not kv_ready:
                kv_h = h
                kv_ready = True
            half = QK_ROPE // 2
            fr = ROPE_THETA ** (-jnp.arange(half, dtype=F32) / half)
            freqs = fr.reshape(half, 1)
            w_kr_p = bf(jnp.pad(w_kr, ((0, 0), (0, LANES - QK_ROPE))))
            assert depth - n_a == 1
            wq = w_uq[j].reshape(cfg.q_lora, hds, QK_NOPE + QK_ROPE)
            wq = jnp.pad(wq, ((0, 0), (0, 0), (0, QK_PAD - QK_NOPE - QK_ROPE))).reshape(cfg.q_lora, hds * QK_PAD)
            q, k, vt = _mla_proj(
                kv_h, row(kv_in_norm), row(ln_mix[i]), bf(w_dkv), w_kr_p, bf(w_dq[j]), row(ckv_norm),
                _pad_lanes(k_rope_norm), row(cq_norm[j]), positions.reshape(1, m), freqs,
                bf(w_uk), bf(w_uv.T), bf(wq.T), row(k_nope_norm),
                q_nope_norm[j].reshape(-1, 1), q_rope_norm[j].reshape(-1, 1), cfg, t)
            o = _attn(q, k, vt, cfg, t)
            h = _matmul_res(o.reshape(m, hds * V_HEAD), bf(mla_out_w), j, h, t, "mla_out_proj")
        h = _mlp(h, row(ln_mlp[i]), mlp_w1, mlp_w2, i, t)
        h = _ple(h, row(ple_norm[i]), bf(ple_gate_w[i]), p2, i, bf(ple_proj_w[i]), t)
    return h.reshape(cfg.batch, cfg.seq, d)


_CFG = _Cfg(batch=2, seq=4096, d_model=2048, d_ff=8192, ple_dim=256, d_inner=4096, ssm_groups=8,
            mla_heads=16, q_lora=512, kv_lora=512)
_TILES = _Tiles(tm_proj=1024, tm_out=512, tn_proj=512, tn_in=1024, tm_mlp=1024, tf_mlp=512, tm_ple=512, tm_mla=512, t_attn=512,
                attn_heads=4, up_heads=2, vmem_mb=60)


def kernel(x, p, positions, ln_mix, ln_mlp, mlp_w1, mlp_w2, ple_norm, ple_gate_w, ple_proj_w, ssm_in_w, ssm_conv_w, ssm_conv_b, ssm_dt_bias, ssm_a_log, ssm_d, ssm_norm, ssm_out_w, kv_in_norm, w_dkv, ckv_norm, w_kr, w_uk, w_uv, k_nope_norm, k_rope_norm, w_dq, cq_norm, w_uq, q_nope_norm, q_rope_norm, mla_out_w):
    return _forward(_CFG, _TILES, x, p, positions, ln_mix, ln_mlp, mlp_w1, mlp_w2, ple_norm, ple_gate_w,
                    ple_proj_w, ssm_in_w, ssm_conv_w, ssm_conv_b, ssm_dt_bias, ssm_a_log, ssm_d, ssm_norm,
                    ssm_out_w, kv_in_norm, w_dkv, ckv_norm, w_kr, w_uk, w_uv, k_nope_norm, k_rope_norm,
                    w_dq, cq_norm, w_uq, q_nope_norm, q_rope_norm, mla_out_w)
```
